```python
import jax
import jax.numpy as jnp
from jax import lax
import numpy as np

D_MODEL = 1024
BATCH = 8
SEQ = 2048
DEPTH = 2
DEC_BATCH = 32
DEC_SEQ = 8
PAST_LEN = 16384
PAGE_SIZE = 128

HEAD_DIM = 64
H_FOX = 4
H_NSA = 8
G_NSA = 2
R_NSA = H_NSA // G_NSA
H_STK = 4
MIX_WIDTH = (H_FOX + H_NSA + H_STK) * HEAD_DIM
CMP_BLOCK = 64
CMP_HIDDEN = 256
TOP_N = 16
WINDOW = 512
FORCE_BONUS = 1.0e4
FORGET_BIAS_INIT = 3.0
D_FF = -(-8 * D_MODEL // (3 * 256)) * 256
QBLOCK = 128
NSA_QBLOCK = 32
ROPE_THETA = 10000.0
EPS = 1e-6
_COLS = (
    ('fox_q', H_FOX * HEAD_DIM), ('fox_k', H_FOX * HEAD_DIM), ('fox_v', H_FOX * HEAD_DIM), ('fox_f', H_FOX),
    ('nsa_q', H_NSA * HEAD_DIM), ('nsa_kc', G_NSA * HEAD_DIM), ('nsa_vc', G_NSA * HEAD_DIM),
    ('nsa_ks', G_NSA * HEAD_DIM), ('nsa_vs', G_NSA * HEAD_DIM), ('nsa_kw', G_NSA * HEAD_DIM),
    ('nsa_vw', G_NSA * HEAD_DIM), ('nsa_g', 3 * H_NSA),
    ('stk_q', H_STK * HEAD_DIM), ('stk_k', H_STK * HEAD_DIM), ('stk_v', H_STK * HEAD_DIM),
)
IN_WIDTH = 3 * H_FOX * HEAD_DIM + H_FOX + (H_NSA + 6 * G_NSA) * HEAD_DIM + 3 * H_NSA + 3 * H_STK * HEAD_DIM

kernel_name = 'hybrid_fox_nsa_stick_decode_step'


def _rms(x, g):
    xf = x.astype(jnp.float32)
    y = xf * lax.rsqrt(jnp.mean(xf * xf, axis=-1, keepdims=True) + EPS)
    return (y * g.astype(jnp.float32)).astype(x.dtype)


def _rope(x, pos):
    half = HEAD_DIM // 2
    inv = ROPE_THETA ** (-jnp.arange(half, dtype=jnp.float32) / half)
    ang = pos.astype(jnp.float32)[:, None] * inv[None, :]
    cos = jnp.cos(ang)[None, :, None, :]
    sin = jnp.sin(ang)[None, :, None, :]
    xf = x.astype(jnp.float32)
    x1, x2 = xf[..., :half], xf[..., half:]
    return jnp.concatenate([x1 * cos - x2 * sin, x2 * cos + x1 * sin], axis=-1).astype(x.dtype)


def _masked_softmax(s, mask, axis=-1):
    s = jnp.where(mask, s.astype(jnp.float32), -jnp.inf)
    m = jnp.max(s, axis=axis, keepdims=True)
    m = jnp.where(jnp.isfinite(m), m, 0.0)
    e = jnp.where(mask, jnp.exp(s - m), 0.0)
    return e / jnp.maximum(jnp.sum(e, axis=axis, keepdims=True), 1e-30)


def _pad_time(x, n):
    return jnp.pad(x, ((0, 0), (0, n - x.shape[1])) + ((0, 0),) * (x.ndim - 2))


def _map_query_blocks(fn, q_len, qb):
    nblk = -(-q_len // qb)
    out = lax.map(fn, jnp.arange(nblk, dtype=jnp.int32) * qb)
    out = jnp.moveaxis(out, 0, 1)
    out = out.reshape((out.shape[0], nblk * qb) + out.shape[3:])
    return out[:, :q_len]


def _heads(x, h):
    return x.reshape(x.shape[:-1] + (h, HEAD_DIM))


def _split_proj(proj):
    out = {}
    off = 0
    for name, width in _COLS:
        out[name] = proj[..., off:off + width]
        off += width
    return out


def _fox_attention(q, k, v, cum, q_pos0):
    B, Tq = q.shape[:2]
    L = k.shape[1]
    qb = min(QBLOCK, Tq)
    tqp = -(-Tq // qb) * qb
    qp = _pad_time(q, tqp)
    cum_q = _pad_time(cum[:, q_pos0:q_pos0 + Tq], tqp)
    cum_k = jnp.swapaxes(cum, 1, 2)
    kpos = jnp.arange(L, dtype=jnp.int32)
    scale = HEAD_DIM ** -0.5

    def block(start):
        qblk = lax.dynamic_slice_in_dim(qp, start, qb, 1)
        cq = lax.dynamic_slice_in_dim(cum_q, start, qb, 1)
        tpos = q_pos0 + start + jnp.arange(qb, dtype=jnp.int32)
        s = jnp.einsum('bqhd,bkhd->bhqk', qblk, k, preferred_element_type=jnp.float32) * scale
        s = s + jnp.swapaxes(cq, 1, 2)[..., None] - cum_k[:, :, None, :]
        p = _masked_softmax(s, (kpos[None, :] <= tpos[:, None])[None, None])
        return jnp.einsum('bhqk,bkhd->bqhd', p.astype(v.dtype), v)

    return _map_query_blocks(block, Tq, qb)


def _stick_attention(q, k, v, q_pos0):
    B, Tq = q.shape[:2]
    L = k.shape[1]
    qb = min(QBLOCK, Tq)
    qp = _pad_time(q, -(-Tq // qb) * qb)
    kpos = jnp.arange(L, dtype=jnp.int32)
    scale = HEAD_DIM ** -0.5

    def block(start):
        qblk = lax.dynamic_slice_in_dim(qp, start, qb, 1)
        tpos = q_pos0 + start + jnp.arange(qb, dtype=jnp.int32)
        z = jnp.einsum('bqhd,bkhd->bhqk', qblk, k, preferred_element_type=jnp.float32) * scale
        mask = (kpos[None, :] < tpos[:, None])[None, None]
        log_keep = jnp.where(mask, jax.nn.log_sigmoid(-z), 0.0)
        log_a = jax.nn.log_sigmoid(z) + lax.cumsum(log_keep, axis=3, reverse=True) - log_keep
        a = jnp.where(mask, jnp.exp(log_a), 0.0)
        return jnp.einsum('bhqk,bkhd->bqhd', a.astype(v.dtype), v)

    return _map_query_blocks(block, Tq, qb)


def _nsa_compress(x, pe, w1, w2):
    B, Lp, G, D = x.shape
    nb = Lp // CMP_BLOCK
    blk = x.reshape(B, nb, CMP_BLOCK, G, D) + pe[:, None, :].astype(x.dtype)
    flat = jnp.transpose(blk, (0, 1, 3, 2, 4)).reshape(B, nb, G, CMP_BLOCK * D)
    h = jax.nn.gelu(jnp.einsum('bngi,ih->bngh', flat, w1))
    return jnp.einsum('bngh,hd->bngd', h, w2)


def _window_attention(q, k, v, q_pos0, k_pos0):
    B, Tq = q.shape[:2]
    Lk = k.shape[1]
    qb = min(QBLOCK, Tq)
    tqp = -(-Tq // qb) * qb
    qp = _pad_time(q, tqp)
    tail = max(0, q_pos0 + tqp - k_pos0 - Lk)
    padw = ((0, 0), (WINDOW, tail), (0, 0), (0, 0))
    kp = jnp.pad(k, padw)
    vp = jnp.pad(v, padw)
    span = WINDOW + qb
    offs = jnp.arange(span, dtype=jnp.int32)
    scale = HEAD_DIM ** -0.5

    def block(start):
        qblk = lax.dynamic_slice_in_dim(qp, start, qb, 1)
        p0 = q_pos0 + start
        kb = lax.dynamic_slice_in_dim(kp, p0 - k_pos0, span, 1)
        vb = lax.dynamic_slice_in_dim(vp, p0 - k_pos0, span, 1)
        kpos = p0 - WINDOW + offs
        tpos = p0 + jnp.arange(qb, dtype=jnp.int32)
        mask = ((kpos[None, :] <= tpos[:, None]) & (kpos[None, :] > tpos[:, None] - WINDOW)
                & (kpos[None, :] >= k_pos0))
        s = jnp.einsum('bqgrd,bkgd->bgrqk', qblk, kb, preferred_element_type=jnp.float32) * scale
        p = _masked_softmax(s, mask[None, None, None])
        return jnp.einsum('bgrqk,bkgd->bqgrd', p.astype(vb.dtype), vb)

    return _map_query_blocks(block, Tq, qb)


def _nsa_attention(q, kc_raw, vc_raw, ks, vs, kw, vw, gate, q_pos0, kw_pos0, lw):
    B, Tq = q.shape[:2]
    L = ks.shape[1]
    nb = -(-L // CMP_BLOCK)
    lp = nb * CMP_BLOCK
    bend = jnp.arange(nb, dtype=jnp.int32) * CMP_BLOCK + (CMP_BLOCK - 1)
    kc = _nsa_compress(_pad_time(kc_raw, lp), lw['nsa_pe_k'], lw['nsa_w1k'], lw['nsa_w2k'])
    kc = _rope(_rms(kc, lw['nsa_gkc']), bend)
    vc = _nsa_compress(_pad_time(vc_raw, lp), lw['nsa_pe_v'], lw['nsa_w1v'], lw['nsa_w2v'])
    ks_blk = jnp.transpose(_pad_time(ks, lp).reshape(B, nb, CMP_BLOCK, G_NSA, HEAD_DIM), (0, 3, 1, 2, 4))
    vs_blk = jnp.transpose(_pad_time(vs, lp).reshape(B, nb, CMP_BLOCK, G_NSA, HEAD_DIM), (0, 3, 1, 2, 4))
    qg = q.reshape(B, Tq, G_NSA, R_NSA, HEAD_DIM)
    n_sel = min(TOP_N, nb)
    qb = min(NSA_QBLOCK, Tq)
    qgp = _pad_time(qg, -(-Tq // qb) * qb)
    bi = jnp.arange(B)[:, None, None, None]
    gi = jnp.arange(G_NSA)[None, :, None, None]
    bidx = jnp.arange(nb, dtype=jnp.int32)
    scale = HEAD_DIM ** -0.5

    def block(start):
        qblk = lax.dynamic_slice_in_dim(qgp, start, qb, 1)
        tpos = q_pos0 + start + jnp.arange(qb, dtype=jnp.int32)
        sc = jnp.einsum('bqgrd,bngd->bgrqn', qblk, kc, preferred_element_type=jnp.float32) * scale
        pc = _masked_softmax(sc, (bend[None, :] <= tpos[:, None])[None, None, None])
        o_c = jnp.einsum('bgrqn,bngd->bqgrd', pc.astype(vc.dtype), vc)
        cur = tpos // CMP_BLOCK
        forced = ((bidx[None, :] == 0) | (bidx[None, :] == cur[:, None])
                  | (bidx[None, :] == cur[:, None] - 1))
        visible = bidx[None, :] * CMP_BLOCK <= tpos[:, None]
        score = jnp.where(visible, jnp.sum(pc, axis=2) + FORCE_BONUS * forced, -jnp.inf)
        top_val, top_idx = lax.top_k(score, n_sel)
        kg = ks_blk[bi, gi, top_idx]
        vg = vs_blk[bi, gi, top_idx]
        ss = jnp.einsum('bqgrd,bgqnkd->bgrqnk', qblk, kg, preferred_element_type=jnp.float32) * scale
        kpos = top_idx[..., None] * CMP_BLOCK + jnp.arange(CMP_BLOCK, dtype=jnp.int32)
        smask = jnp.isfinite(top_val)[..., None] & (kpos <= tpos[None, None, :, None, None])
        ps = _masked_softmax(ss, smask[:, :, None], axis=(-2, -1))
        o_s = jnp.einsum('bgrqnk,bgqnkd->bqgrd', ps.astype(vg.dtype), vg)
        return jnp.stack([o_c, o_s], axis=4)

    o_cs = _map_query_blocks(block, Tq, qb)
    o_w = _window_attention(qg, kw, vw, q_pos0, kw_pos0)
    g = gate.reshape(B, Tq, G_NSA, R_NSA, 3)
    o = g[..., 0:1] * o_cs[..., 0, :] + g[..., 1:2] * o_cs[..., 1, :] + g[..., 2:3] * o_w
    return o.reshape(B, Tq, H_NSA * HEAD_DIM)


def _mixer_rows(xn, pos, lw):
    B, T, _ = xn.shape
    c = _split_proj(jnp.einsum('btd,dn->btn', xn, lw['w_in']))
    logf = jax.nn.log_sigmoid(c['fox_f'].astype(jnp.float32) + lw['b_fox_f'].astype(jnp.float32))
    return {
        'fox_q': _rms(_heads(c['fox_q'], H_FOX), lw['fox_gq']),
        'fox_k': _rms(_heads(c['fox_k'], H_FOX), lw['fox_gk']),
        'fox_v': _heads(c['fox_v'], H_FOX),
        'fox_logf': logf.astype(xn.dtype),
        'nsa_q': _rope(_rms(_heads(c['nsa_q'], H_NSA), lw['nsa_gq']), pos),
        'nsa_kc': _heads(c['nsa_kc'], G_NSA),
        'nsa_vc': _heads(c['nsa_vc'], G_NSA),
        'nsa_ks': _rope(_rms(_heads(c['nsa_ks'], G_NSA), lw['nsa_gks']), pos),
        'nsa_vs': _heads(c['nsa_vs'], G_NSA),
        'nsa_kw': _rope(_rms(_heads(c['nsa_kw'], G_NSA), lw['nsa_gkw']), pos),
        'nsa_vw': _heads(c['nsa_vw'], G_NSA),
        'nsa_gate': jax.nn.sigmoid(c['nsa_g'] + lw['b_nsa_gate']).reshape(B, T, H_NSA, 3),
        'stk_q': _heads(c['stk_q'], H_STK),
        'stk_k': _heads(c['stk_k'], H_STK),
        'stk_v': _heads(c['stk_v'], H_STK),
    }


def _mix_out(o_fox, o_nsa, o_stk, g_mix, w_o):
    B, T = o_nsa.shape[:2]
    wf = H_FOX * HEAD_DIM
    wn = H_NSA * HEAD_DIM
    o = jnp.concatenate([
        _rms(o_fox.reshape(B, T, wf), g_mix[:wf]),
        _rms(o_nsa, g_mix[wf:wf + wn]),
        _rms(o_stk.reshape(B, T, H_STK * HEAD_DIM), g_mix[wf + wn:]),
    ], axis=-1)
    return jnp.einsum('btm,md->btd', o, w_o)


def _swiglu(h, w_gate, w_up, w_down):
    return jnp.einsum('btf,fd->btd', jax.nn.silu(h @ w_gate) * (h @ w_up), w_down)


def _layer(x, pos0, lw, past):
    B, T, _ = x.shape
    pos = pos0 + jnp.arange(T, dtype=jnp.int32)
    r = _mixer_rows(_rms(x, lw['g_attn']), pos, lw)
    fox_rows = jnp.stack([r['fox_k'], r['fox_v']], axis=2)
    logf_rows = r['fox_logf']
    cmp_rows = jnp.stack([r['nsa_kc'], r['nsa_vc']], axis=2)
    sel_rows = jnp.stack([r['nsa_ks'], r['nsa_vs']], axis=2)
    win_rows = jnp.stack([r['nsa_kw'], r['nsa_vw']], axis=2)
    stk_rows = jnp.stack([r['stk_k'], r['stk_v']], axis=2)
    if past is None:
        fox_kv, logf, cmp_kv, sel_kv, stk_kv, win_kv = fox_rows, logf_rows, cmp_rows, sel_rows, stk_rows, win_rows
        n_win = min(WINDOW, T)
    else:
        fox_kv = jnp.concatenate([past['fox_kv'], fox_rows], axis=1)
        logf = jnp.concatenate([past['fox_logf'], logf_rows], axis=1)
        cmp_kv = jnp.concatenate([past['cmp_kv'], cmp_rows], axis=1)
        sel_kv = jnp.concatenate([past['sel_kv'], sel_rows], axis=1)
        stk_kv = jnp.concatenate([past['stk_kv'], stk_rows], axis=1)
        win_kv = jnp.concatenate([past['win_kv'], win_rows], axis=1)
        n_win = past['win_kv'].shape[1]
    win_pos0 = pos0 + T - win_kv.shape[1]
    cum = jnp.cumsum(logf.astype(jnp.float32), axis=1)
    o_fox = _fox_attention(r['fox_q'], fox_kv[:, :, 0], fox_kv[:, :, 1], cum, pos0)
    o_nsa = _nsa_attention(r['nsa_q'], cmp_kv[:, :, 0], cmp_kv[:, :, 1], sel_kv[:, :, 0], sel_kv[:, :, 1],
                           win_kv[:, :, 0], win_kv[:, :, 1], r['nsa_gate'], pos0, win_pos0, lw)
    o_stk = _stick_attention(r['stk_q'], stk_kv[:, :, 0], stk_kv[:, :, 1], pos0)
    x = x + _mix_out(o_fox, o_nsa, o_stk, lw['g_mix'], lw['w_o'])
    x = x + _swiglu(_rms(x, lw['g_ffn']), lw['w_gate'], lw['w_up'], lw['w_down'])
    new_win = win_kv[:, win_kv.shape[1] - n_win:]
    return x, (fox_rows, logf_rows, cmp_rows, sel_rows, new_win, stk_rows)


def _gather_pages(cache, l, page_table):
    g = cache[l, page_table]
    return g.reshape((g.shape[0], g.shape[1] * g.shape[2]) + g.shape[3:])


def setup_inputs(seed: int = 0) -> dict:
    key = jax.random.key(seed)
    k = jax.random.split(key, 32)
    n_pages = PAST_LEN // PAGE_SIZE
    n_pool = (5 * DEC_BATCH * n_pages) // 4
    n_win = min(WINDOW, PAST_LEN)
    hd = HEAD_DIM

    def nrm(kk, shape, scale=1.0):
        return jax.random.normal(kk, shape, jnp.float32) * scale

    def gain(kk, shape):
        return 1.0 + 0.02 * jax.random.normal(kk, shape, jnp.float32)

    page_table = jax.random.permutation(k[8], n_pool)[:DEC_BATCH * n_pages].reshape(DEC_BATCH, n_pages).astype(jnp.int32)
    return {
        'x_prompt': nrm(k[0], (BATCH, SEQ, D_MODEL)),
        'x_sample': nrm(k[1], (DEC_BATCH, DEC_SEQ, D_MODEL)),
        'cache_fox_kv': nrm(k[2], (DEPTH, n_pool, PAGE_SIZE, 2, H_FOX, hd)),
        'cache_fox_logf': jax.nn.log_sigmoid(FORGET_BIAS_INIT + nrm(k[3], (DEPTH, n_pool, PAGE_SIZE, H_FOX))),
        'cache_nsa_cmp_kv': nrm(k[4], (DEPTH, n_pool, PAGE_SIZE, 2, G_NSA, hd)),
        'cache_nsa_sel_kv': nrm(k[5], (DEPTH, n_pool, PAGE_SIZE, 2, G_NSA, hd)),
        'state_nsa_win_kv': nrm(k[6], (DEPTH, DEC_BATCH, n_win, 2, G_NSA, hd)),
        'cache_stk_kv': nrm(k[7], (DEPTH, n_pool, PAGE_SIZE, 2, H_STK, hd)),
        'page_table': page_table,
        'g_attn': gain(k[9], (DEPTH, D_MODEL)),
        'w_in': nrm(k[10], (DEPTH, D_MODEL, IN_WIDTH), D_MODEL ** -0.5),
        'b_fox_f': FORGET_BIAS_INIT + nrm(k[11], (DEPTH, H_FOX), 0.1),
        'b_nsa_gate': nrm(k[12], (DEPTH, 3 * H_NSA), 0.02),
        'fox_gq': gain(k[13], (DEPTH, hd)),
        'fox_gk': gain(k[14], (DEPTH, hd)),
        'nsa_gq': gain(k[15], (DEPTH, hd)),
        'nsa_gkc': gain(k[16], (DEPTH, hd)),
        'nsa_gks': gain(k[17], (DEPTH, hd)),
        'nsa_gkw': gain(k[18], (DEPTH, hd)),
        'nsa_pe_k': nrm(k[19], (DEPTH, CMP_BLOCK, hd), 0.1),
        'nsa_w1k': nrm(k[20], (DEPTH, CMP_BLOCK * hd, CMP_HIDDEN), (CMP_BLOCK * hd) ** -0.5),
        'nsa_w2k': nrm(k[21], (DEPTH, CMP_HIDDEN, hd), CMP_HIDDEN ** -0.5),
        'nsa_pe_v': nrm(k[22], (DEPTH, CMP_BLOCK, hd), 0.1),
        'nsa_w1v': nrm(k[23], (DEPTH, CMP_BLOCK * hd, CMP_HIDDEN), (CMP_BLOCK * hd) ** -0.5),
        'nsa_w2v': nrm(k[24], (DEPTH, CMP_HIDDEN, hd), CMP_HIDDEN ** -0.5),
        'g_mix': gain(k[25], (DEPTH, MIX_WIDTH)),
        'w_o': nrm(k[26], (DEPTH, MIX_WIDTH, D_MODEL), MIX_WIDTH ** -0.5),
        'g_ffn': gain(k[27], (DEPTH, D_MODEL)),
        'w_gate': nrm(k[28], (DEPTH, D_MODEL, D_FF), D_MODEL ** -0.5),
        'w_up': nrm(k[29], (DEPTH, D_MODEL, D_FF), D_MODEL ** -0.5),
        'w_down': nrm(k[30], (DEPTH, D_FF, D_MODEL), D_FF ** -0.5),
    }


def reference(x_prompt, x_sample, cache_fox_kv, cache_fox_logf, cache_nsa_cmp_kv, cache_nsa_sel_kv,
              state_nsa_win_kv, cache_stk_kv, page_table, g_attn, w_in, b_fox_f, b_nsa_gate,
              fox_gq, fox_gk, nsa_gq, nsa_gkc, nsa_gks, nsa_gkw, nsa_pe_k, nsa_w1k, nsa_w2k,
              nsa_pe_v, nsa_w1v, nsa_w2v, g_mix, w_o, g_ffn, w_gate, w_up, w_down):
    past_len = page_table.shape[1] * PAGE_SIZE
    xp, xs = x_prompt, x_sample
    rows_p, rows_s = [], []
    for l in range(DEPTH):
        lw = {
            'g_attn': g_attn[l], 'w_in': w_in[l], 'b_fox_f': b_fox_f[l], 'b_nsa_gate': b_nsa_gate[l],
            'fox_gq': fox_gq[l], 'fox_gk': fox_gk[l], 'nsa_gq': nsa_gq[l], 'nsa_gkc': nsa_gkc[l],
            'nsa_gks': nsa_gks[l], 'nsa_gkw': nsa_gkw[l], 'nsa_pe_k': nsa_pe_k[l], 'nsa_w1k': nsa_w1k[l],
            'nsa_w2k': nsa_w2k[l], 'nsa_pe_v': nsa_pe_v[l], 'nsa_w1v': nsa_w1v[l], 'nsa_w2v': nsa_w2v[l],
            'g_mix': g_mix[l], 'w_o': w_o[l], 'g_ffn': g_ffn[l], 'w_gate': w_gate[l], 'w_up': w_up[l],
            'w_down': w_down[l],
        }
        xp, rp = _layer(xp, 0, lw, None)
        past = {
            'fox_kv': _gather_pages(cache_fox_kv, l, page_table),
            'fox_logf': _gather_pages(cache_fox_logf, l, page_table),
            'cmp_kv': _gather_pages(cache_nsa_cmp_kv, l, page_table),
            'sel_kv': _gather_pages(cache_nsa_sel_kv, l, page_table),
            'stk_kv': _gather_pages(cache_stk_kv, l, page_table),
            'win_kv': state_nsa_win_kv[l],
        }
        xs, rs = _layer(xs, past_len, lw, past)
        rows_p.append(rp)
        rows_s.append(rs)
    fox_kv_p = jnp.stack([r[0] for r in rows_p], axis=0)
    fox_kv_s = jnp.stack([r[0] for r in rows_s], axis=0)
    fox_logf_p = jnp.stack([r[1] for r in rows_p], axis=0)
    fox_logf_s = jnp.stack([r[1] for r in rows_s], axis=0)
    cmp_kv_p = jnp.stack([r[2] for r in rows_p], axis=0)
    cmp_kv_s = jnp.stack([r[2] for r in rows_s], axis=0)
    sel_kv_p = jnp.stack([r[3] for r in rows_p], axis=0)
    sel_kv_s = jnp.stack([r[3] for r in rows_s], axis=0)
    win_kv_p = jnp.stack([r[4] for r in rows_p], axis=0)
    win_kv_s = jnp.stack([r[4] for r in rows_s], axis=0)
    stk_kv_p = jnp.stack([r[5] for r in rows_p], axis=0)
    stk_kv_s = jnp.stack([r[5] for r in rows_s], axis=0)
    return (xp, xs, fox_kv_p, fox_kv_s, fox_logf_p, fox_logf_s, cmp_kv_p, cmp_kv_s,
            sel_kv_p, sel_kv_s, win_kv_p, win_kv_s, stk_kv_p, stk_kv_s)
```

```python
import functools
import math

import numpy as np
import jax
import jax.numpy as jnp
from jax import lax
from jax.experimental import pallas as pl
from jax.experimental.pallas import tpu as pltpu

F32 = jnp.float32
BF16 = jnp.bfloat16

HEAD_DIM = 64
H_FOX = 4
H_NSA = 8
G_NSA = 2
R_NSA = H_NSA // G_NSA
H_STK = 4
CMP_BLOCK = 64
CMP_HIDDEN = 256
TOP_N = 16
WINDOW = 512
FORCE_BONUS = 1.0e4
ROPE_THETA = 10000.0
EPS = 1e-6
LANES = 128
SCALE = HEAD_DIM ** -0.5

W_FOX = H_FOX * HEAD_DIM
W_NSA = H_NSA * HEAD_DIM
W_GRP = G_NSA * HEAD_DIM
W_STK = H_STK * HEAD_DIM
N_MISC = H_FOX + 3 * H_NSA

C_FQ, C_FKV, C_NQ, C_CMP, C_SEL, C_WIN, C_SQ, C_SKV, C_MISC = 0, 256, 768, 1280, 1536, 1792, 2048, 2304, 2816
W_PROJ = 2944

NSA_HEAD_ORDER = (0, 4, 1, 5, 2, 6, 3, 7)

VMEM_LIMIT = 56 * 1024 * 1024


def _cparams(*sem):
    return pltpu.CompilerParams(dimension_semantics=sem, vmem_limit_bytes=VMEM_LIMIT)


def _iota(shape, dim):
    return lax.broadcasted_iota(jnp.int32, shape, dim)


def _log_sigmoid(z):
    return jnp.minimum(z, 0.0) - jnp.log1p(jnp.exp(-jnp.abs(z)))


def _dot(a, b):
    return jnp.dot(a, b, preferred_element_type=F32)


def _dot_nt(a, b):
    return lax.dot_general(a, b, (((1,), (1,)), ((), ())), preferred_element_type=F32)


def _split3(x):
    x1 = x.astype(BF16)
    r1 = x - x1.astype(F32)
    x2 = r1.astype(BF16)
    x3 = (r1 - x2.astype(F32)).astype(BF16)
    return x1, x2, x3


def _dot_exact_rhs(x, e):
    x1, x2, x3 = _split3(x)
    return _dot(x1, e) + _dot(x2, e) + _dot(x3, e)


def _dot_exact_lhs(e, x):
    x1, x2, x3 = _split3(x)
    return _dot(e, x1) + _dot(e, x2) + _dot(e, x3)


def _head_norm(y, s64, gain):
    w = y.shape[1]
    y2 = (y * y).astype(BF16)
    outs = []
    for c in range(0, w, 256):
        cw = min(256, w - c)
        ms = _dot(y2[:, c:c + cw], s64[:cw, :cw])
        outs.append(y[:, c:c + cw] * lax.rsqrt(ms + EPS))
    yn = outs[0] if len(outs) == 1 else jnp.concatenate(outs, axis=1)
    return yn * gain


def _rope(y, cos, sin):
    w = y.shape[1]
    lane = _iota(y.shape, 1)
    first = (lane & 63) < 32
    sw = jnp.where(first, pltpu.roll(y, w - 32, 1), pltpu.roll(y, 32, 1))
    if w > LANES:
        cos = jnp.concatenate([cos] * (w // LANES), axis=1)
        sin = jnp.concatenate([sin] * (w // LANES), axis=1)
    return y * cos + sw * sin


def _suffix_excl_lanes(x):
    lane = _iota(x.shape, 1)
    y = x
    s = 1
    while s < LANES:
        y = y + jnp.where(lane + s < LANES, pltpu.roll(y, LANES - s, 1), 0.0)
        s *= 2
    return y - x, y[:, 0:1]


def _stack_heads(q, n_heads):
    lane = _iota(q.shape, 1)
    zero = jnp.zeros_like(q)
    return jnp.concatenate([jnp.where((lane >> 6) == h, q, zero) for h in range(n_heads)], axis=0)


def _unstack_heads(o, n_heads, t):
    lane = _iota((t, o.shape[1]), 1)
    out = jnp.zeros((t, o.shape[1]), F32)
    for h in range(n_heads):
        out = out + jnp.where((lane >> 6) == h, o[h * t:(h + 1) * t], 0.0)
    return out


def _stack_nsa(q):
    t = q.shape[0]
    lane = _iota((t, LANES), 1)
    blocks = []
    for c in range(R_NSA):
        qc = q[:, c * LANES:(c + 1) * LANES]
        for g in range(G_NSA):
            blocks.append(jnp.where((lane >> 6) == g, qc, jnp.zeros_like(qc)))
    return jnp.concatenate(blocks, axis=0)


def _unstack_nsa(o, t):
    lane = _iota((t, LANES), 1)
    chunks = []
    for c in range(R_NSA):
        a = o[(2 * c) * t:(2 * c + 1) * t]
        b = o[(2 * c + 1) * t:(2 * c + 2) * t]
        chunks.append(jnp.where((lane >> 6) == 0, a, b))
    return jnp.concatenate(chunks, axis=1)


def _softmax_update(s, mask, m_ref, l_ref, acc_ref, v):
    if mask is not None:
        s = jnp.where(mask, s, -jnp.inf)
    m_old = m_ref[...]
    m_new = jnp.maximum(m_old, jnp.max(s, axis=1, keepdims=True))
    m_safe = jnp.where(m_new > -jnp.inf, m_new, 0.0)
    alpha = jnp.exp(m_old - m_safe)
    p = jnp.exp(s - m_safe)
    l_ref[...] = alpha * l_ref[...] + jnp.sum(p, axis=1, keepdims=True)
    acc_ref[...] = alpha * acc_ref[...] + _dot(p.astype(BF16), v)
    m_ref[...] = m_new


def _gate_expand(misc, eg_ref):
    m1, m2, m3 = _split3(misc)
    outs = []
    for j in range(3):
        e = eg_ref[j]
        outs.append(_dot(m1, e) + _dot(m2, e) + _dot(m3, e))
    return outs


def _rope_table_kernel(pos_ref, inv_ref, sgn_ref, cos_ref, sin_ref):
    ang = pos_ref[...] * inv_ref[...]
    cos_ref[...] = jnp.cos(ang)
    sin_ref[...] = jnp.sin(ang) * sgn_ref[...]


def _rope_tables(pos):
    p = pos.shape[0]
    half = HEAD_DIM // 2
    inv = ROPE_THETA ** (-jnp.arange(half, dtype=F32) / half)
    inv128 = jnp.tile(inv, 4)[None, :]
    sgn128 = jnp.tile(jnp.concatenate([-jnp.ones((half,), F32), jnp.ones((half,), F32)]), 2)[None, :]
    return pl.pallas_call(
        _rope_table_kernel,
        out_shape=(jax.ShapeDtypeStruct((p, LANES), F32), jax.ShapeDtypeStruct((p, LANES), F32)),
        name="rope_tables",
    )(pos.astype(F32)[:, None], inv128, sgn128)


def _proj_kernel(x_ref, g_ref, w_ref, bias_ref, gains_ref, cos_ref, sin_ref, s64_ref,
                 qf_ref, fkv_ref, qn_ref, cmp_ref, sel_ref, win_ref, qs_ref, skv_ref, misc_ref):
    x = x_ref[...]
    ms = jnp.mean(x * x, axis=-1, keepdims=True)
    xn = (x * lax.rsqrt(ms + EPS) * g_ref[...]).astype(BF16)
    cos = cos_ref[...]
    sin = sin_ref[...]
    s64 = s64_ref[...]

    def mm(c0, w):
        return _dot(xn, w_ref[:, c0:c0 + w])

    qf_ref[...] = (_head_norm(mm(C_FQ, W_FOX), s64, gains_ref[0:1, :W_FOX]) * SCALE).astype(BF16)
    kv = mm(C_FKV, 2 * W_FOX)
    fkv_ref[:, :W_FOX] = _head_norm(kv[:, :W_FOX], s64, gains_ref[1:2, :W_FOX])
    fkv_ref[:, W_FOX:] = kv[:, W_FOX:]
    qn = _rope(_head_norm(mm(C_NQ, W_NSA), s64, gains_ref[2:3, :W_NSA]), cos, sin)
    qn_ref[...] = (qn * SCALE).astype(BF16)
    cmp_ref[...] = mm(C_CMP, 2 * W_GRP)
    sel = mm(C_SEL, 2 * W_GRP)
    sel_ref[:, :W_GRP] = _rope(_head_norm(sel[:, :W_GRP], s64, gains_ref[3:4, :W_GRP]), cos, sin)
    sel_ref[:, W_GRP:] = sel[:, W_GRP:]
    win = mm(C_WIN, 2 * W_GRP)
    win_ref[:, :W_GRP] = _rope(_head_norm(win[:, :W_GRP], s64, gains_ref[4:5, :W_GRP]), cos, sin)
    win_ref[:, W_GRP:] = win[:, W_GRP:]
    qs_ref[...] = (mm(C_SQ, W_STK) * SCALE).astype(BF16)
    skv_ref[...] = mm(C_SKV, 2 * W_STK)
    c = mm(C_MISC, LANES) + bias_ref[...]
    lane = _iota(c.shape, 1)
    misc_ref[...] = jnp.where(lane < H_FOX, _log_sigmoid(c), 1.0 / (1.0 + jnp.exp(-c)))


def _proj(x2d, lw, cos, sin, n_pos_blocks, tn):
    n, d = x2d.shape
    nt = n // tn
    row = lambda i: (i, 0)
    const = lambda i: (0, 0)
    widths = (W_FOX, 2 * W_FOX, W_NSA, 2 * W_GRP, 2 * W_GRP, 2 * W_GRP, W_STK, 2 * W_STK, LANES)
    dtypes = (BF16, F32, BF16, F32, F32, F32, BF16, F32, F32)
    out_shape = tuple(jax.ShapeDtypeStruct((n, w), dt) for w, dt in zip(widths, dtypes))
    out_specs = tuple(pl.BlockSpec((tn, w), row) for w in widths)
    pos_map = lambda i: (i % n_pos_blocks, 0)
    return pl.pallas_call(
        _proj_kernel,
        grid=(nt,),
        in_specs=[
            pl.BlockSpec((tn, d), row),
            pl.BlockSpec((1, d), const),
            pl.BlockSpec((d, W_PROJ), const),
            pl.BlockSpec((1, LANES), const),
            pl.BlockSpec((8, W_NSA), const),
            pl.BlockSpec((tn, LANES), pos_map),
            pl.BlockSpec((tn, LANES), pos_map),
            pl.BlockSpec((256, 256), const),
        ],
        out_specs=out_specs,
        out_shape=out_shape,
        compiler_params=_cparams("arbitrary"),
        name="proj",
    )(x2d, lw["g_attn"], lw["w_in"], lw["bias_misc"], lw["gains"], cos, sin, lw["s64"])


def _cumsum_kernel(misc_ref, ccol_ref, crow_ref, *, blk):
    t = misc_ref.shape[1]
    r = _iota((blk, blk), 0)
    c = _iota((blk, blk), 1)
    tri = jnp.where(r >= c, 1.0, 0.0).astype(BF16)
    carry = jnp.zeros((1, LANES), F32)
    for i in range(t // blk):
        x = misc_ref[0, i * blk:(i + 1) * blk, :]
        cs = _dot_exact_lhs(tri, x) + carry
        carry = cs[blk - 1:blk, :]
        ccol_ref[0, i * blk:(i + 1) * blk, :] = cs
        crow_ref[0, i] = cs.T[0:8, :]


def _cumsum(misc3, blk):
    b, t, _ = misc3.shape
    return pl.pallas_call(
        functools.partial(_cumsum_kernel, blk=blk),
        grid=(b,),
        in_specs=[pl.BlockSpec((1, t, LANES), lambda i: (i, 0, 0))],
        out_specs=(pl.BlockSpec((1, t, LANES), lambda i: (i, 0, 0)),
                   pl.BlockSpec((1, t // blk, 8, blk), lambda i: (i, 0, 0, 0))),
        out_shape=(jax.ShapeDtypeStruct((b, t, LANES), F32),
                   jax.ShapeDtypeStruct((b, t // blk, 8, blk), F32)),
        compiler_params=_cparams("arbitrary"),
        name="logf_cumsum",
    )(misc3)


def _fox_prompt_kernel(q_ref, kv_ref, ccol_ref, crow_ref, o_ref, kb, vb, m_s, l_s, acc_s, *, tq, tk):
    qi = pl.program_id(1)

    @pl.when(qi == 0)
    def _():
        kb[...] = kv_ref[0, :, :W_FOX].astype(BF16)
        vb[...] = kv_ref[0, :, W_FOX:].astype(BF16)

    q4 = _stack_heads(q_ref[0], H_FOX)
    cc = ccol_ref[0]
    cq = jnp.concatenate([cc[:, h:h + 1] for h in range(H_FOX)], axis=0)
    tpos = qi * tq + _iota((tq, 1), 0)
    tpos4 = jnp.concatenate([tpos] * H_FOX, axis=0)
    m_s[...] = jnp.full(m_s.shape, -jnp.inf, F32)
    l_s[...] = jnp.zeros(l_s.shape, F32)
    acc_s[...] = jnp.zeros(acc_s.shape, F32)

    def body(kj, carry):
        k0 = pl.multiple_of(kj * tk, tk)
        k = kb[pl.ds(k0, tk), :]
        v = vb[pl.ds(k0, tk), :]
        s = _dot_nt(q4, k)
        cr = crow_ref[0, kj]
        ck = jnp.concatenate([jnp.broadcast_to(cr[h:h + 1, :], (tq, tk)) for h in range(H_FOX)], axis=0)
        s = s + cq - ck
        kpos = k0 + _iota((1, tk), 1)
        _softmax_update(s, kpos <= tpos4, m_s, l_s, acc_s, v)
        return carry

    nk = (qi * tq + tq - 1) // tk + 1
    lax.fori_loop(0, nk, body, 0)
    o4 = acc_s[...] / jnp.maximum(l_s[...], 1e-30)
    o_ref[0] = _unstack_heads(o4, H_FOX, tq)


def _fox_prompt(q3, kv3, ccol, crow, tq, tk):
    b, t, _ = q3.shape
    return pl.pallas_call(
        functools.partial(_fox_prompt_kernel, tq=tq, tk=tk),
        grid=(b, t // tq),
        in_specs=[
            pl.BlockSpec((1, tq, W_FOX), lambda i, j: (i, j, 0)),
            pl.BlockSpec((1, t, 2 * W_FOX), lambda i, j: (i, 0, 0)),
            pl.BlockSpec((1, tq, LANES), lambda i, j: (i, j, 0)),
            pl.BlockSpec((1, t // tk, 8, tk), lambda i, j: (i, 0, 0, 0)),
        ],
        out_specs=pl.BlockSpec((1, tq, W_FOX), lambda i, j: (i, j, 0)),
        out_shape=jax.ShapeDtypeStruct((b, t, W_FOX), F32),
        scratch_shapes=[
            pltpu.VMEM((t, W_FOX), BF16), pltpu.VMEM((t, W_FOX), BF16),
            pltpu.VMEM((H_FOX * tq, 1), F32), pltpu.VMEM((H_FOX * tq, 1), F32),
            pltpu.VMEM((H_FOX * tq, W_FOX), F32),
        ],
        compiler_params=_cparams("arbitrary", "arbitrary"),
        name="fox_prompt",
    )(q3, kv3, ccol, crow)


def _stick_tile(z, mask, r_col, tri):
    ls = _log_sigmoid(z)
    lk = ls - z
    if mask is not None:
        lk = jnp.where(mask, lk, 0.0)
    hi = lk.astype(BF16)
    lo = (lk - hi.astype(F32)).astype(BF16)
    rc = _dot(hi, tri) + _dot(lo, tri)
    a = jnp.exp(ls + rc + r_col)
    if mask is not None:
        a = jnp.where(mask, a, 0.0)
    return a, r_col + jnp.sum(lk, axis=1, keepdims=True)


def _stick_prompt_kernel(q_ref, kv_ref, o_ref, kb, vb, r_s, acc_s, *, tq, tk):
    qi = pl.program_id(1)

    @pl.when(qi == 0)
    def _():
        kb[...] = kv_ref[0, :, :W_STK].astype(BF16)
        vb[...] = kv_ref[0, :, W_STK:].astype(BF16)

    q4 = _stack_heads(q_ref[0], H_STK)
    tpos = qi * tq + _iota((tq, 1), 0)
    tpos4 = jnp.concatenate([tpos] * H_STK, axis=0)
    tri = jnp.where(_iota((tk, tk), 0) > _iota((tk, tk), 1), 1.0, 0.0).astype(BF16)
    r_s[...] = jnp.zeros(r_s.shape, F32)
    acc_s[...] = jnp.zeros(acc_s.shape, F32)
    nk = (qi * tq + tq - 1) // tk + 1

    def body(it, carry):
        kj = nk - 1 - it
        k0 = pl.multiple_of(kj * tk, tk)
        k = kb[pl.ds(k0, tk), :]
        v = vb[pl.ds(k0, tk), :]
        z = _dot_nt(q4, k)
        kpos = k0 + _iota((1, tk), 1)
        a, r_new = _stick_tile(z, kpos < tpos4, r_s[...], tri)
        acc_s[...] = acc_s[...] + _dot(a.astype(BF16), v)
        r_s[...] = r_new
        return carry

    lax.fori_loop(0, nk, body, 0)
    o_ref[0] = _unstack_heads(acc_s[...], H_STK, tq)


def _stick_prompt(q3, kv3, tq, tk):
    b, t, _ = q3.shape
    return pl.pallas_call(
        functools.partial(_stick_prompt_kernel, tq=tq, tk=tk),
        grid=(b, t // tq),
        in_specs=[
            pl.BlockSpec((1, tq, W_STK), lambda i, j: (i, j, 0)),
            pl.BlockSpec((1, t, 2 * W_STK), lambda i, j: (i, 0, 0)),
        ],
        out_specs=pl.BlockSpec((1, tq, W_STK), lambda i, j: (i, j, 0)),
        out_shape=jax.ShapeDtypeStruct((b, t, W_STK), F32),
        scratch_shapes=[
            pltpu.VMEM((t, W_STK), BF16), pltpu.VMEM((t, W_STK), BF16),
            pltpu.VMEM((H_STK * tq, 1), F32), pltpu.VMEM((H_STK * tq, W_STK), F32),
        ],
        compiler_params=_cparams("arbitrary", "arbitrary"),
        name="stick_prompt",
    )(q3, kv3)


def _compress_body(x_refs, pe_ref, w1k_ref, w1v_ref, w2k_ref, w2v_ref, gkc_ref, cos_ref, sin_ref, s64_ref,
                   kc_ref, vc_ref, xs):
    off = 0
    for r in x_refs:
        rows = r.shape[-2]
        v = r[...].reshape(rows, 2 * W_GRP)
        xs[0, off:off + rows, :] = v[:, :LANES]
        xs[1, off:off + rows, :] = v[:, LANES:]
        off += rows
    nblk = off // CMP_BLOCK
    lane = _iota((nblk, LANES), 1)
    lo_half = lane < HEAD_DIM
    hk = jnp.zeros((2 * nblk, CMP_HIDDEN), F32)
    hv = jnp.zeros((2 * nblk, CMP_HIDDEN), F32)
    for i4 in range(CMP_BLOCK // 4):
        x = [[xs[h, pl.ds(4 * i4 + j, nblk, stride=CMP_BLOCK), :] for h in range(2)] for j in range(4)]
        sw = [[pltpu.roll(v, HEAD_DIM, 1) for v in xj] for xj in x]

        def chunk_at(j, c, pos):
            return x[j][c // 2] if (c % 2) == pos else sw[j][c // 2]

        packed = []
        for c in range(4):
            left = jnp.where(lo_half, chunk_at(0, c, 0), chunk_at(1, c, 1))
            right = jnp.where(lo_half, chunk_at(2, c, 0), chunk_at(3, c, 1))
            packed.append(jnp.concatenate([left, right], axis=1))
        ak = jnp.concatenate([packed[0], packed[1]], axis=0) + pe_ref[0, i4:i4 + 1, :]
        av = jnp.concatenate([packed[2], packed[3]], axis=0) + pe_ref[1, i4:i4 + 1, :]
        hk = hk + _dot(ak.astype(BF16), w1k_ref[256 * i4:256 * (i4 + 1), :])
        hv = hv + _dot(av.astype(BF16), w1v_ref[256 * i4:256 * (i4 + 1), :])
    gk = jax.nn.gelu(hk).astype(BF16)
    gv = jax.nn.gelu(hv).astype(BF16)
    kc = _dot(gk[:nblk], w2k_ref[0]) + _dot(gk[nblk:], w2k_ref[1])
    kc = _rope(_head_norm(kc, s64_ref[...], gkc_ref[...]), cos_ref[...], sin_ref[...])
    kc_ref[...] = kc.reshape(kc_ref.shape)
    vc = _dot(gv[:nblk], w2v_ref[0]) + _dot(gv[nblk:], w2v_ref[1])
    vc_ref[...] = vc.reshape(vc_ref.shape)


def _compress_kernel(*refs, n_x, n_prefetch):
    refs = refs[n_prefetch:]
    _compress_body(refs[:n_x], *refs[n_x:])


def _compress_weight_specs(nmap):
    return [
        pl.BlockSpec((2, CMP_BLOCK // 4, 256), nmap(3)),
        pl.BlockSpec((CMP_BLOCK * HEAD_DIM, CMP_HIDDEN), nmap(2)),
        pl.BlockSpec((CMP_BLOCK * HEAD_DIM, CMP_HIDDEN), nmap(2)),
        pl.BlockSpec((2, CMP_HIDDEN, LANES), nmap(3)),
        pl.BlockSpec((2, CMP_HIDDEN, LANES), nmap(3)),
        pl.BlockSpec((1, LANES), nmap(2)),
    ]


def _compress_contig(x3, lw, cos, sin):
    b, t, _ = x3.shape
    nblk = t // CMP_BLOCK
    nmap = lambda n: (lambda i: (0,) * n)
    return pl.pallas_call(
        functools.partial(_compress_kernel, n_x=1, n_prefetch=0),
        grid=(b,),
        in_specs=[pl.BlockSpec((1, t, 2 * W_GRP), lambda i: (i, 0, 0))] + _compress_weight_specs(nmap) + [
            pl.BlockSpec((nblk, LANES), nmap(2)),
            pl.BlockSpec((nblk, LANES), nmap(2)),
            pl.BlockSpec((256, 256), nmap(2)),
        ],
        out_specs=(pl.BlockSpec((1, nblk, LANES), lambda i: (i, 0, 0)),
                   pl.BlockSpec((1, nblk, LANES), lambda i: (i, 0, 0))),
        out_shape=(jax.ShapeDtypeStruct((b, nblk, LANES), F32), jax.ShapeDtypeStruct((b, nblk, LANES), F32)),
        scratch_shapes=[pltpu.VMEM((2, t, LANES), F32)],
        compiler_params=_cparams("arbitrary"),
        name="nsa_compress",
    )(x3, lw["pe4"], lw["w1k"], lw["w1v"], lw["w2k"], lw["w2v"], lw["gkc"], cos, sin, lw["s64"])


def _compress_paged(cache4, layer, page_table, lw, cos, sin, g_pages):
    db, n_pages = page_table.shape
    page = cache4.shape[2]
    ns = n_pages // g_pages
    nblk = g_pages * page // CMP_BLOCK
    nmap = lambda n: (lambda b, s, pt: (0,) * n)

    def page_spec(i):
        return pl.BlockSpec((None, None, page, 2 * W_GRP),
                            lambda b, s, pt, i=i: (layer, pt[b, s * g_pages + i], 0, 0))

    grid_spec = pltpu.PrefetchScalarGridSpec(
        num_scalar_prefetch=1,
        grid=(db, ns),
        in_specs=[page_spec(i) for i in range(g_pages)] + _compress_weight_specs(nmap) + [
            pl.BlockSpec((nblk, LANES), lambda b, s, pt: (s, 0)),
            pl.BlockSpec((nblk, LANES), lambda b, s, pt: (s, 0)),
            pl.BlockSpec((256, 256), nmap(2)),
        ],
        out_specs=(pl.BlockSpec((1, nblk, LANES), lambda b, s, pt: (b, s, 0)),
                   pl.BlockSpec((1, nblk, LANES), lambda b, s, pt: (b, s, 0))),
        scratch_shapes=[pltpu.VMEM((2, g_pages * page, LANES), F32)],
    )
    nb_past = n_pages * page // CMP_BLOCK
    return pl.pallas_call(
        functools.partial(_compress_kernel, n_x=g_pages, n_prefetch=1),
        grid_spec=grid_spec,
        out_shape=(jax.ShapeDtypeStruct((db, nb_past, LANES), F32), jax.ShapeDtypeStruct((db, nb_past, LANES), F32)),
        compiler_params=_cparams("arbitrary", "arbitrary"),
        name="nsa_compress_paged",
    )(page_table, *([cache4] * g_pages), lw["pe4"], lw["w1k"], lw["w1v"], lw["w2k"], lw["w2v"], lw["gkc"],
      cos, sin, lw["s64"])


def _block_select(score, visible, lane_n, nb, n_sel, group_lanes):
    cnt = jnp.zeros(score.shape, F32)
    n_slots = score.shape[1] // group_lanes
    lane = _iota(score.shape, 1)
    for j in range(nb):
        if n_slots == 1:
            vj = score[:, j:j + 1]
        else:
            vj = score[:, j:j + 1]
            for sl in range(1, n_slots):
                vj = jnp.where(lane >= sl * group_lanes, score[:, sl * group_lanes + j:sl * group_lanes + j + 1], vj)
        ahead = (vj > score) | ((vj == score) & (lane_n > j))
        cnt = cnt + jnp.where(ahead, 1.0, 0.0)
    return visible & (cnt < n_sel)


def _masked_softmax_rows(s, mask):
    s = jnp.where(mask, s, -jnp.inf)
    m = jnp.max(s, axis=1, keepdims=True)
    m = jnp.where(m > -jnp.inf, m, 0.0)
    e = jnp.where(mask, jnp.exp(s - m), 0.0)
    return e / jnp.maximum(jnp.sum(e, axis=1, keepdims=True), 1e-30)


def _nsa_prompt_kernel(q_ref, kc_ref, vc_ref, sel_ref, win_ref, misc_ref, eg_ref, eb_ref, o_ref,
                       selb, winb, kcs, vcs, mexp, m_s, l_s, acc_s, *, tq, tk, nb, span):
    qi = pl.program_id(1)
    t_all = sel_ref.shape[1]

    @pl.when(qi == 0)
    def _():
        selb[...] = sel_ref[0].astype(BF16)
        winb[...] = win_ref[0].astype(BF16)
        lane = _iota((nb, LANES), 1)
        kc = kc_ref[0]
        vc = vc_ref[0]
        kcs[...] = jnp.zeros(kcs.shape, BF16)
        vcs[...] = jnp.zeros(vcs.shape, BF16)
        for g in range(G_NSA):
            keep = (lane >> 6) == g
            kcs[g * HEAD_DIM:g * HEAD_DIM + nb, :] = jnp.where(keep, kc, 0.0).astype(BF16)
            vcs[g * HEAD_DIM:g * HEAD_DIM + nb, :] = jnp.where(keep, vc, 0.0).astype(BF16)

    q = q_ref[0]
    tpos = qi * tq + _iota((tq, 1), 0)
    lane = _iota((tq, LANES), 1)
    n_idx = lane & 63
    half = lane >> 6

    cmask = (n_idx * CMP_BLOCK + (CMP_BLOCK - 1) <= tpos) & (n_idx < nb)
    imp = jnp.zeros((tq, LANES), F32)
    oc_chunks = []
    for c in range(R_NSA):
        s = _dot_nt(q[:, c * LANES:(c + 1) * LANES], kcs[...])
        s = jnp.where(cmask, s, -jnp.inf)
        m0 = jnp.max(jnp.where(half == 0, s, -jnp.inf), axis=1, keepdims=True)
        m1 = jnp.max(jnp.where(half == 1, s, -jnp.inf), axis=1, keepdims=True)
        m = jnp.where(half == 0, m0, m1)
        m = jnp.where(m > -jnp.inf, m, 0.0)
        e = jnp.where(cmask, jnp.exp(s - m), 0.0)
        d0 = jnp.sum(jnp.where(half == 0, e, 0.0), axis=1, keepdims=True)
        d1 = jnp.sum(jnp.where(half == 1, e, 0.0), axis=1, keepdims=True)
        pc = e / jnp.maximum(jnp.where(half == 0, d0, d1), 1e-30)
        imp = imp + pc
        oc_chunks.append(_dot(pc.astype(BF16), vcs[...]))
    o_c = jnp.concatenate(oc_chunks, axis=1)

    visible = (n_idx * CMP_BLOCK <= tpos) & (n_idx < nb)
    cur = tpos // CMP_BLOCK
    forced = (n_idx == 0) | (n_idx == cur) | (n_idx == cur - 1)
    score = jnp.where(visible, imp + FORCE_BONUS * jnp.where(forced, 1.0, 0.0), -jnp.inf)
    selected = _block_select(score, visible, n_idx, nb, min(TOP_N, nb), HEAD_DIM)
    sel_b = jnp.where(selected, 1.0, 0.0).astype(BF16)
    nkt = t_all // tk
    for g in range(G_NSA):
        me = _dot(sel_b, eb_ref[g])
        for kt in range(nkt):
            mexp[g, kt] = me[:, kt * tk:(kt + 1) * tk]

    qs = _stack_nsa(q)
    tpos8 = jnp.concatenate([tpos] * (2 * R_NSA), axis=0)

    m_s[...] = jnp.full(m_s.shape, -jnp.inf, F32)
    l_s[...] = jnp.zeros(l_s.shape, F32)
    acc_s[...] = jnp.zeros(acc_s.shape, F32)

    def body(kj, carry):
        k0 = pl.multiple_of(kj * tk, tk)
        k = selb[pl.ds(k0, tk), :W_GRP]
        v = selb[pl.ds(k0, tk), W_GRP:]
        s = _dot_nt(qs, k)
        mg = [mexp[g, kj] > 0.5 for g in range(G_NSA)]
        msel = jnp.concatenate([mg[0], mg[1]] * R_NSA, axis=0)
        kpos = k0 + _iota((1, tk), 1)
        _softmax_update(s, msel & (kpos <= tpos8), m_s, l_s, acc_s, v)
        return carry

    nk = (qi * tq + tq - 1) // tk + 1
    lax.fori_loop(0, nk, body, 0)
    o_s = _unstack_nsa(acc_s[...] / jnp.maximum(l_s[...], 1e-30), tq)

    start = pl.multiple_of(jnp.maximum(qi * tq + tq - span, 0), LANES)
    kw = winb[pl.ds(start, span), :W_GRP]
    vw = winb[pl.ds(start, span), W_GRP:]
    sw = _dot_nt(qs, kw)
    kpos = start + _iota((1, span), 1)
    pw = _masked_softmax_rows(sw, (kpos <= tpos8) & (kpos > tpos8 - WINDOW))
    o_w = _unstack_nsa(_dot(pw.astype(BF16), vw), tq)

    g_c, g_s, g_w = _gate_expand(misc_ref[0], eg_ref)
    o_ref[0] = g_c * o_c + g_s * o_s + g_w * o_w


def _nsa_prompt(q3, kc, vc, sel3, win3, misc3, consts, tq, tk):
    b, t, _ = q3.shape
    nb = t // CMP_BLOCK
    span = min(WINDOW + tq, t)
    eb = consts["eb_prompt"]
    return pl.pallas_call(
        functools.partial(_nsa_prompt_kernel, tq=tq, tk=tk, nb=nb, span=span),
        grid=(b, t // tq),
        in_specs=[
            pl.BlockSpec((1, tq, W_NSA), lambda i, j: (i, j, 0)),
            pl.BlockSpec((1, nb, LANES), lambda i, j: (i, 0, 0)),
            pl.BlockSpec((1, nb, LANES), lambda i, j: (i, 0, 0)),
            pl.BlockSpec((1, t, 2 * W_GRP), lambda i, j: (i, 0, 0)),
            pl.BlockSpec((1, t, 2 * W_GRP), lambda i, j: (i, 0, 0)),
            pl.BlockSpec((1, tq, LANES), lambda i, j: (i, j, 0)),
            pl.BlockSpec((3, LANES, W_NSA), lambda i, j: (0, 0, 0)),
            pl.BlockSpec((G_NSA, LANES, t), lambda i, j: (0, 0, 0)),
        ],
        out_specs=pl.BlockSpec((1, tq, W_NSA), lambda i, j: (i, j, 0)),
        out_shape=jax.ShapeDtypeStruct((b, t, W_NSA), F32),
        scratch_shapes=[
            pltpu.VMEM((t, 2 * W_GRP), BF16), pltpu.VMEM((t, 2 * W_GRP), BF16),
            pltpu.VMEM((LANES, LANES), BF16), pltpu.VMEM((LANES, LANES), BF16),
            pltpu.VMEM((G_NSA, t // tk, tq, tk), F32),
            pltpu.VMEM((H_NSA * tq, 1), F32), pltpu.VMEM((H_NSA * tq, 1), F32),
            pltpu.VMEM((H_NSA * tq, LANES), F32),
        ],
        compiler_params=_cparams("arbitrary", "arbitrary"),
        name="nsa_prompt",
    )(q3, kc, vc, sel3, win3, misc3, consts["egate"], eb)


def _fox_decode_kernel(pt_ref, *refs, g_pages, page):
    kv_refs = refs[:g_pages]
    lf_refs = refs[g_pages:2 * g_pages]
    q_ref, knew_ref, lfn_ref, o_ref, m_s, l_s, acc_s, carry_s, nc_s = refs[2 * g_pages:]
    step = pl.program_id(1)
    tqn = q_ref.shape[1]
    rows = H_FOX * tqn
    q4 = _stack_heads(q_ref[0], H_FOX)
    tqv = _iota((rows, 1), 0) & (tqn - 1)

    def expand_heads(x):
        return jnp.concatenate([jnp.broadcast_to(x[h:h + 1, :], (tqn, x.shape[1])) for h in range(H_FOX)], axis=0)

    @pl.when(step == 0)
    def _():
        lfn = lfn_ref[0]
        lane8 = _iota(lfn.shape, 1)
        cum = lfn
        s = 1
        while s < tqn:
            cum = cum + jnp.where(lane8 >= s, pltpu.roll(cum, s, 1), 0.0)
            s *= 2
        e = expand_heads(cum)
        lane = _iota(e.shape, 1)
        nc = jnp.sum(jnp.where(lane == tqv, e, 0.0), axis=1, keepdims=True)
        nc_s[...] = nc
        carry_s[...] = jnp.zeros(carry_s.shape, F32)
        m_s[...] = jnp.full(m_s.shape, -jnp.inf, F32)
        l_s[...] = jnp.zeros(l_s.shape, F32)
        acc_s[...] = jnp.zeros(acc_s.shape, F32)
        kn = knew_ref[0]
        sc = _dot_nt(q4, kn[:, :W_FOX].astype(BF16)) + nc - e
        _softmax_update(sc, lane <= tqv, m_s, l_s, acc_s, kn[:, W_FOX:].astype(BF16))

    k = jnp.concatenate([r[:, :W_FOX].astype(BF16) for r in kv_refs], axis=0)
    v = jnp.concatenate([r[:, W_FOX:].astype(BF16) for r in kv_refs], axis=0)
    sc = _dot_nt(q4, k)
    carry = carry_s[...]
    xs = []
    for r in lf_refs:
        excl, tot = _suffix_excl_lanes(r[...])
        xs.append(excl + carry)
        carry = carry + tot
    carry_s[...] = carry
    bias = expand_heads(jnp.concatenate(xs, axis=1))
    _softmax_update(sc + nc_s[...] + bias, None, m_s, l_s, acc_s, v)

    @pl.when(step == pl.num_programs(1) - 1)
    def _():
        o_ref[0] = _unstack_heads(acc_s[...] / jnp.maximum(l_s[...], 1e-30), H_FOX, tqn)


def _fox_decode(q3, cache4, logf4, layer, page_table, knew, lfn, g_pages):
    db, n_pages = page_table.shape
    page = cache4.shape[2]
    tqn = q3.shape[1]
    ns = n_pages // g_pages
    rows = H_FOX * tqn

    def pg(b, s, pt, i):
        return pt[b, n_pages - 1 - (s * g_pages + i)]

    kv_specs = [pl.BlockSpec((None, None, page, 2 * W_FOX), lambda b, s, pt, i=i: (layer, pg(b, s, pt, i), 0, 0))
                for i in range(g_pages)]
    lf_specs = [pl.BlockSpec((None, None, H_FOX, page), lambda b, s, pt, i=i: (layer, pg(b, s, pt, i), 0, 0))
                for i in range(g_pages)]
    grid_spec = pltpu.PrefetchScalarGridSpec(
        num_scalar_prefetch=1,
        grid=(db, ns),
        in_specs=kv_specs + lf_specs + [
            pl.BlockSpec((1, tqn, W_FOX), lambda b, s, pt: (b, 0, 0)),
            pl.BlockSpec((1, page, 2 * W_FOX), lambda b, s, pt: (b, 0, 0)),
            pl.BlockSpec((1, 8, LANES), lambda b, s, pt: (b, 0, 0)),
        ],
        out_specs=pl.BlockSpec((1, tqn, W_FOX), lambda b, s, pt: (b, 0, 0)),
        scratch_shapes=[
            pltpu.VMEM((rows, 1), F32), pltpu.VMEM((rows, 1), F32), pltpu.VMEM((rows, W_FOX), F32),
            pltpu.VMEM((H_FOX, 1), F32), pltpu.VMEM((rows, 1), F32),
        ],
    )
    return pl.pallas_call(
        functools.partial(_fox_decode_kernel, g_pages=g_pages, page=page),
        grid_spec=grid_spec,
        out_shape=jax.ShapeDtypeStruct((db, tqn, W_FOX), F32),
        compiler_params=_cparams("arbitrary", "arbitrary"),
        name="fox_decode",
    )(page_table, *([cache4] * g_pages), *([logf4] * g_pages), q3, knew, lfn)


def _stick_decode_kernel(pt_ref, *refs, g_pages, page):
    kv_refs = refs[:g_pages]
    q_ref, knew_ref, o_ref, r_s, acc_s = refs[g_pages:]
    step = pl.program_id(1)
    tqn = q_ref.shape[1]
    rows = H_STK * tqn
    q4 = _stack_heads(q_ref[0], H_STK)
    tqv = _iota((rows, 1), 0) & (tqn - 1)

    def sweep(z, mask, r_col, n_chunks):
        ls = _log_sigmoid(z)
        lk = ls - z
        if mask is not None:
            lk = jnp.where(mask, lk, 0.0)
        rcs = []
        for i in range(n_chunks):
            excl, tot = _suffix_excl_lanes(lk[:, i * LANES:(i + 1) * LANES])
            rcs.append(excl + r_col)
            r_col = r_col + tot
        rc = rcs[0] if n_chunks == 1 else jnp.concatenate(rcs, axis=1)
        a = jnp.exp(ls + rc)
        if mask is not None:
            a = jnp.where(mask, a, 0.0)
        return a, r_col

    @pl.when(step == 0)
    def _():
        kn = knew_ref[0]
        z = _dot_nt(q4, kn[:, :W_STK].astype(BF16))
        lane = _iota(z.shape, 1)
        a, r_col = sweep(z, lane < tqv, jnp.zeros((rows, 1), F32), page // LANES)
        r_s[...] = r_col
        acc_s[...] = _dot(a.astype(BF16), kn[:, W_STK:].astype(BF16))

    k = jnp.concatenate([r[:, :W_STK].astype(BF16) for r in kv_refs], axis=0)
    v = jnp.concatenate([r[:, W_STK:].astype(BF16) for r in kv_refs], axis=0)
    a, r_col = sweep(_dot_nt(q4, k), None, r_s[...], g_pages * page // LANES)
    r_s[...] = r_col
    acc_s[...] = acc_s[...] + _dot(a.astype(BF16), v)

    @pl.when(step == pl.num_programs(1) - 1)
    def _():
        o_ref[0] = _unstack_heads(acc_s[...], H_STK, tqn)


def _stick_decode(q3, cache4, layer, page_table, knew, g_pages):
    db, n_pages = page_table.shape
    page = cache4.shape[2]
    tqn = q3.shape[1]
    ns = n_pages // g_pages
    rows = H_STK * tqn
    kv_specs = [pl.BlockSpec((None, None, page, 2 * W_STK),
                             lambda b, s, pt, i=i: (layer, pt[b, n_pages - 1 - (s * g_pages + i)], 0, 0))
                for i in range(g_pages)]
    grid_spec = pltpu.PrefetchScalarGridSpec(
        num_scalar_prefetch=1,
        grid=(db, ns),
        in_specs=kv_specs + [
            pl.BlockSpec((1, tqn, W_STK), lambda b, s, pt: (b, 0, 0)),
            pl.BlockSpec((1, page, 2 * W_STK), lambda b, s, pt: (b, 0, 0)),
        ],
        out_specs=pl.BlockSpec((1, tqn, W_STK), lambda b, s, pt: (b, 0, 0)),
        scratch_shapes=[pltpu.VMEM((rows, 1), F32), pltpu.VMEM((rows, W_STK), F32)],
    )
    return pl.pallas_call(
        functools.partial(_stick_decode_kernel, g_pages=g_pages, page=page),
        grid_spec=grid_spec,
        out_shape=jax.ShapeDtypeStruct((db, tqn, W_STK), F32),
        compiler_params=_cparams("arbitrary", "arbitrary"),
        name="stick_decode",
    )(page_table, *([cache4] * g_pages), q3, knew)


def _nsa_sample_small_kernel(q_ref, kc_ref, vc_ref, win_ref, oc_ref, ow_ref, selm_ref, *, nb, pos0, win_pos0):
    q = q_ref[0]
    tqn = q.shape[0]
    nbp = kc_ref.shape[1]
    qs = _stack_nsa(q)
    rows = qs.shape[0]
    tpos = pos0 + (_iota((rows, 1), 0) & (tqn - 1))
    kc = kc_ref[0].astype(BF16)
    vc = vc_ref[0].astype(BF16)
    n_idx = _iota((rows, nbp), 1)
    cmask = (n_idx * CMP_BLOCK + (CMP_BLOCK - 1) <= tpos) & (n_idx < nb)
    pc = _masked_softmax_rows(_dot_nt(qs, kc), cmask)
    oc_ref[0] = _unstack_nsa(_dot(pc.astype(BF16), vc), tqn)

    imp = []
    for g in range(G_NSA):
        acc = jnp.zeros((tqn, nbp), F32)
        for c in range(R_NSA):
            blk = 2 * c + g
            acc = acc + pc[blk * tqn:(blk + 1) * tqn]
        imp.append(acc)
    imp = jnp.concatenate(imp, axis=0)
    tp2 = pos0 + (_iota((G_NSA * tqn, 1), 0) & (tqn - 1))
    n2 = _iota(imp.shape, 1)
    visible = (n2 * CMP_BLOCK <= tp2) & (n2 < nb)
    cur = tp2 // CMP_BLOCK
    forced = (n2 == 0) | (n2 == cur) | (n2 == cur - 1)
    score = jnp.where(visible, imp + FORCE_BONUS * jnp.where(forced, 1.0, 0.0), -jnp.inf)
    selected = _block_select(score, visible, n2, nb, min(TOP_N, nb), nbp)
    selm_ref[0] = jnp.where(selected, 1.0, 0.0)

    kw = win_ref[0, :, :W_GRP].astype(BF16)
    vw = win_ref[0, :, W_GRP:].astype(BF16)
    kpos = win_pos0 + _iota((1, kw.shape[0]), 1)
    pw = _masked_softmax_rows(_dot_nt(qs, kw), (kpos <= tpos) & (kpos > tpos - WINDOW) & (kpos >= win_pos0))
    ow_ref[0] = _unstack_nsa(_dot(pw.astype(BF16), vw), tqn)


def _nsa_sample_small(q3, kc, vc, win, nb, pos0, win_pos0):
    db, tqn, _ = q3.shape
    nbp = kc.shape[1]
    wk = win.shape[1]
    return pl.pallas_call(
        functools.partial(_nsa_sample_small_kernel, nb=nb, pos0=pos0, win_pos0=win_pos0),
        grid=(db,),
        in_specs=[
            pl.BlockSpec((1, tqn, W_NSA), lambda b: (b, 0, 0)),
            pl.BlockSpec((1, nbp, LANES), lambda b: (b, 0, 0)),
            pl.BlockSpec((1, nbp, LANES), lambda b: (b, 0, 0)),
            pl.BlockSpec((1, wk, 2 * W_GRP), lambda b: (b, 0, 0)),
        ],
        out_specs=(pl.BlockSpec((1, tqn, W_NSA), lambda b: (b, 0, 0)),
                   pl.BlockSpec((1, tqn, W_NSA), lambda b: (b, 0, 0)),
                   pl.BlockSpec((1, G_NSA * tqn, nbp), lambda b: (b, 0, 0))),
        out_shape=(jax.ShapeDtypeStruct((db, tqn, W_NSA), F32), jax.ShapeDtypeStruct((db, tqn, W_NSA), F32),
                   jax.ShapeDtypeStruct((db, G_NSA * tqn, nbp), F32)),
        compiler_params=_cparams("arbitrary"),
        name="nsa_sample_small",
    )(q3, kc, vc, win)


def _sel_decode_kernel(pt_ref, *refs, g_pages, page, nb_new):
    kv_refs = refs[:g_pages]
    (q_ref, knew_ref, msel_ref, mnew_ref, ee_ref, oc_ref, ow_ref, misc_ref, eg_ref,
     o_ref, m_s, l_s, acc_s) = refs[g_pages:]
    step = pl.program_id(1)
    tqn = q_ref.shape[1]
    qs = _stack_nsa(q_ref[0])
    rows = qs.shape[0]
    tqv = _iota((rows, 1), 0) & (tqn - 1)

    @pl.when(step == 0)
    def _():
        m_s[...] = jnp.full(m_s.shape, -jnp.inf, F32)
        l_s[...] = jnp.zeros(l_s.shape, F32)
        acc_s[...] = jnp.zeros(acc_s.shape, F32)
        kn = knew_ref[0]
        s = _dot_nt(qs, kn[:, :W_GRP].astype(BF16))
        lane = _iota(s.shape, 1)
        mn = mnew_ref[0][:, nb_new:nb_new + 1] > 0.5
        mn = jnp.concatenate([mn] * R_NSA, axis=0)
        _softmax_update(s, mn & (lane <= tqv), m_s, l_s, acc_s, kn[:, W_GRP:].astype(BF16))

    k = jnp.concatenate([r[:, :W_GRP].astype(BF16) for r in kv_refs], axis=0)
    v = jnp.concatenate([r[:, W_GRP:].astype(BF16) for r in kv_refs], axis=0)
    s = _dot_nt(qs, k)
    me = _dot(msel_ref[0, 0].astype(BF16), ee_ref[...]) > 0.5
    _softmax_update(s, jnp.concatenate([me] * R_NSA, axis=0), m_s, l_s, acc_s, v)

    @pl.when(step == pl.num_programs(1) - 1)
    def _():
        o_s = _unstack_nsa(acc_s[...] / jnp.maximum(l_s[...], 1e-30), tqn)
        g_c, g_s, g_w = _gate_expand(misc_ref[0], eg_ref)
        o_ref[0] = g_c * oc_ref[0] + g_s * o_s + g_w * ow_ref[0]


def _sel_decode(q3, cache4, layer, page_table, knew, selm, o_c, o_w, misc3, consts, g_pages):
    db, n_pages = page_table.shape
    page = cache4.shape[2]
    tqn = q3.shape[1]
    ns = n_pages // g_pages
    rows = H_NSA * tqn
    nbp = selm.shape[2]
    bps = g_pages * page // CMP_BLOCK
    nb_past = n_pages * page // CMP_BLOCK
    msel = selm[:, :, :nb_past].reshape(db, G_NSA * tqn, ns, bps).transpose(0, 2, 1, 3)
    kv_specs = [pl.BlockSpec((None, None, page, 2 * W_GRP),
                             lambda b, s, pt, i=i: (layer, pt[b, s * g_pages + i], 0, 0))
                for i in range(g_pages)]
    per_b = lambda b, s, pt: (b, 0, 0)
    grid_spec = pltpu.PrefetchScalarGridSpec(
        num_scalar_prefetch=1,
        grid=(db, ns),
        in_specs=kv_specs + [
            pl.BlockSpec((1, tqn, W_NSA), per_b),
            pl.BlockSpec((1, page, 2 * W_GRP), per_b),
            pl.BlockSpec((1, 1, G_NSA * tqn, bps), lambda b, s, pt: (b, s, 0, 0)),
            pl.BlockSpec((1, G_NSA * tqn, nbp), per_b),
            pl.BlockSpec((bps, g_pages * page), lambda b, s, pt: (0, 0)),
            pl.BlockSpec((1, tqn, W_NSA), per_b),
            pl.BlockSpec((1, tqn, W_NSA), per_b),
            pl.BlockSpec((1, tqn, LANES), per_b),
            pl.BlockSpec((3, LANES, W_NSA), lambda b, s, pt: (0, 0, 0)),
        ],
        out_specs=pl.BlockSpec((1, tqn, W_NSA), per_b),
        scratch_shapes=[pltpu.VMEM((rows, 1), F32), pltpu.VMEM((rows, 1), F32), pltpu.VMEM((rows, LANES), F32)],
    )
    return pl.pallas_call(
        functools.partial(_sel_decode_kernel, g_pages=g_pages, page=page, nb_new=nb_past),
        grid_spec=grid_spec,
        out_shape=jax.ShapeDtypeStruct((db, tqn, W_NSA), F32),
        compiler_params=_cparams("arbitrary", "arbitrary"),
        name="nsa_sel_decode",
    )(page_table, *([cache4] * g_pages), q3, knew, msel, selm, consts["ee_decode"], o_c, o_w, misc3,
      consts["egate"])


def _mix_ffn_kernel(x_ref, of_ref, on_ref, os_ref, gmix_ref, wo_ref, gffn_ref, wg_ref, wu_ref, wd_ref, o_ref,
                    x1_s, xn_s, acc_s):
    j = pl.program_id(1)

    def rms(y, g):
        return y * lax.rsqrt(jnp.mean(y * y, axis=-1, keepdims=True) + EPS) * g

    @pl.when(j == 0)
    def _():
        o = jnp.concatenate([
            rms(of_ref[...], gmix_ref[:, :W_FOX]),
            rms(on_ref[...], gmix_ref[:, W_FOX:W_FOX + W_NSA]),
            rms(os_ref[...], gmix_ref[:, W_FOX + W_NSA:]),
        ], axis=1).astype(BF16)
        x1 = x_ref[...] + _dot(o, wo_ref[...])
        x1_s[...] = x1
        xn_s[...] = rms(x1, gffn_ref[...]).astype(BF16)
        acc_s[...] = jnp.zeros(acc_s.shape, F32)

    xn = xn_s[...]
    hg = _dot(xn, wg_ref[...])
    hu = _dot(xn, wu_ref[...])
    h = (hg * (1.0 / (1.0 + jnp.exp(-hg))) * hu).astype(BF16)
    acc_s[...] = acc_s[...] + _dot(h, wd_ref[...])

    @pl.when(j == pl.num_programs(1) - 1)
    def _():
        o_ref[...] = x1_s[...] + acc_s[...]


def _mix_ffn(x2d, o_fox, o_nsa, o_stk, lw, tn, tf):
    n, d = x2d.shape
    dff = lw["w_gate"].shape[1]
    row = lambda i, j: (i, 0)
    const = lambda i, j: (0, 0)
    return pl.pallas_call(
        _mix_ffn_kernel,
        grid=(n // tn, dff // tf),
        in_specs=[
            pl.BlockSpec((tn, d), row),
            pl.BlockSpec((tn, W_FOX), row),
            pl.BlockSpec((tn, W_NSA), row),
            pl.BlockSpec((tn, W_STK), row),
            pl.BlockSpec((1, d), const),
            pl.BlockSpec((d, d), const),
            pl.BlockSpec((1, d), const),
            pl.BlockSpec((d, tf), lambda i, j: (0, j)),
            pl.BlockSpec((d, tf), lambda i, j: (0, j)),
            pl.BlockSpec((tf, d), lambda i, j: (j, 0)),
        ],
        out_specs=pl.BlockSpec((tn, d), row),
        out_shape=jax.ShapeDtypeStruct((n, d), F32),
        scratch_shapes=[pltpu.VMEM((tn, d), F32), pltpu.VMEM((tn, d), BF16), pltpu.VMEM((tn, d), F32)],
        compiler_params=_cparams("arbitrary", "arbitrary"),
        name="mix_ffn",
    )(x2d, o_fox, o_nsa, o_stk, lw["g_mix"], lw["w_o"], lw["g_ffn"], lw["w_gate"], lw["w_up"], lw["w_down"])


def _nsa_lane_perm():
    perm = []
    for h in NSA_HEAD_ORDER:
        perm.extend(range(h * HEAD_DIM, (h + 1) * HEAD_DIM))
    return np.asarray(perm, np.int32)


def _layer_weights(l, w):
    d = w["w_in"].shape[1]
    perm = _nsa_lane_perm()
    win = w["w_in"][l]
    o_fq, o_fk, o_ff, o_nq = 0, W_FOX, 3 * W_FOX, 3 * W_FOX + H_FOX
    o_kc = o_nq + W_NSA
    o_ng = o_kc + 6 * W_GRP
    o_sq = o_ng + 3 * H_NSA
    nq = win[:, o_nq:o_nq + W_NSA][:, perm]
    w_perm = jnp.concatenate([
        win[:, o_fq:o_fq + W_FOX], win[:, o_fk:o_fk + 2 * W_FOX], nq, win[:, o_kc:o_kc + 6 * W_GRP],
        win[:, o_sq:o_sq + W_STK], win[:, o_sq + W_STK:o_sq + 3 * W_STK],
        win[:, o_ff:o_ff + H_FOX], win[:, o_ng:o_ng + 3 * H_NSA], jnp.zeros((d, LANES - N_MISC), F32),
    ], axis=1).astype(BF16)
    bias = jnp.concatenate([w["b_fox_f"][l], w["b_nsa_gate"][l], jnp.zeros((LANES - N_MISC,), F32)])[None, :]

    def tiled(g, width):
        return jnp.pad(jnp.tile(g, width // HEAD_DIM), (0, W_NSA - width))

    gains = jnp.stack([tiled(w["fox_gq"][l], W_FOX), tiled(w["fox_gk"][l], W_FOX), tiled(w["nsa_gq"][l], W_NSA),
                       tiled(w["nsa_gks"][l], W_GRP), tiled(w["nsa_gkw"][l], W_GRP)]
                      + [jnp.zeros((W_NSA,), F32)] * 3)
    s64 = jnp.asarray(np.kron(np.eye(4, dtype=np.float32), np.full((64, 64), 1.0 / 64, np.float32)), BF16)

    def w2pad(w2):
        z = jnp.zeros_like(w2)
        return jnp.stack([jnp.concatenate([w2, z], axis=1), jnp.concatenate([z, w2], axis=1)]).astype(BF16)

    g_mix = w["g_mix"][l]
    g_mix = jnp.concatenate([g_mix[:W_FOX], g_mix[W_FOX:W_FOX + W_NSA][perm], g_mix[W_FOX + W_NSA:]])[None, :]
    w_o = w["w_o"][l]
    w_o = jnp.concatenate([w_o[:W_FOX], w_o[W_FOX:W_FOX + W_NSA][perm], w_o[W_FOX + W_NSA:]], axis=0).astype(BF16)
    return {
        "g_attn": w["g_attn"][l][None, :], "w_in": w_perm, "bias_misc": bias, "gains": gains, "s64": s64,
        "pe4": jnp.stack([w["nsa_pe_k"][l].reshape(CMP_BLOCK // 4, 256), w["nsa_pe_v"][l].reshape(CMP_BLOCK // 4, 256)]),
        "w1k": w["nsa_w1k"][l].astype(BF16), "w1v": w["nsa_w1v"][l].astype(BF16),
        "w2k": w2pad(w["nsa_w2k"][l]), "w2v": w2pad(w["nsa_w2v"][l]),
        "gkc": jnp.tile(w["nsa_gkc"][l], 2)[None, :],
        "g_mix": g_mix, "w_o": w_o, "g_ffn": w["g_ffn"][l][None, :],
        "w_gate": w["w_gate"][l].astype(BF16), "w_up": w["w_up"][l].astype(BF16), "w_down": w["w_down"][l].astype(BF16),
    }


def _constants(t, g_sel, page):
    perm = _nsa_lane_perm()
    egate = np.zeros((3, LANES, W_NSA), np.float32)
    for lane in range(W_NSA):
        h = int(perm[lane]) // HEAD_DIM
        for j in range(3):
            egate[j, H_FOX + 3 * h + j, lane] = 1.0
    nb = t // CMP_BLOCK
    eb = np.zeros((G_NSA, LANES, t), np.float32)
    for g in range(G_NSA):
        for n in range(nb):
            eb[g, g * HEAD_DIM + n, n * CMP_BLOCK:(n + 1) * CMP_BLOCK] = 1.0
    bps = g_sel * page // CMP_BLOCK
    ee = np.zeros((bps, g_sel * page), np.float32)
    for n in range(bps):
        ee[n, n * CMP_BLOCK:(n + 1) * CMP_BLOCK] = 1.0
    return {"egate": jnp.asarray(egate, BF16), "eb_prompt": jnp.asarray(eb, BF16), "ee_decode": jnp.asarray(ee, BF16)}


def _pick(n, prefs):
    for p in prefs:
        if n % p == 0:
            return p
    return n


def _prompt_layer(x3, lw, tabs, consts):
    b, t, d = x3.shape
    n = b * t
    tn = _pick(t, (256, 128))
    (qf, fkv, qn, cmpr, sel, win, qs, skv, misc) = _proj(x3.reshape(n, d), lw, tabs["cos_p"], tabs["sin_p"], t // tn, tn)
    tq = _pick(t, (256, 128))
    tk = tq
    misc3 = misc.reshape(b, t, LANES)
    ccol, crow = _cumsum(misc3, tk)
    o_fox = _fox_prompt(qf.reshape(b, t, W_FOX), fkv.reshape(b, t, 2 * W_FOX), ccol, crow, tq, tk)
    o_stk = _stick_prompt(qs.reshape(b, t, W_STK), skv.reshape(b, t, 2 * W_STK), tq, tk)
    kc, vc = _compress_contig(cmpr.reshape(b, t, 2 * W_GRP), lw, tabs["cos_bp"], tabs["sin_bp"])
    o_nsa = _nsa_prompt(qn.reshape(b, t, W_NSA), kc, vc, sel.reshape(b, t, 2 * W_GRP), win.reshape(b, t, 2 * W_GRP),
                        misc3, consts, LANES, 2 * LANES if t % (2 * LANES) == 0 else LANES)
    tnf = _pick(n, (512, 256, 128))
    x_out = _mix_ffn(x3.reshape(n, d), o_fox.reshape(n, W_FOX), o_nsa.reshape(n, W_NSA), o_stk.reshape(n, W_STK),
                     lw, tnf, _pick(lw["w_gate"].shape[1], (1408, 1024, 512, 256, 128)))
    n_win = min(WINDOW, t)
    rows = (fkv.reshape(b, t, 2, H_FOX, HEAD_DIM), misc3[:, :, :H_FOX], cmpr.reshape(b, t, 2, G_NSA, HEAD_DIM),
            sel.reshape(b, t, 2, G_NSA, HEAD_DIM), win.reshape(b, t, 2, G_NSA, HEAD_DIM)[:, t - n_win:],
            skv.reshape(b, t, 2, H_STK, HEAD_DIM))
    return x_out.reshape(b, t, d), rows


def _sample_layer(x3, l, lw, tabs, consts, caches, page_table, g_pages):
    db, tqn, d = x3.shape
    n = db * tqn
    n_pages = page_table.shape[1]
    page = caches["fox"].shape[2]
    past = n_pages * page
    (qf, fkv, qn, cmpr, sel, win, qs, skv, misc) = _proj(x3.reshape(n, d), lw, tabs["cos_s"], tabs["sin_s"], 1, n)
    misc3 = misc.reshape(db, tqn, LANES)
    pad_rows = lambda a: jnp.pad(a, ((0, 0), (0, page - tqn), (0, 0)))

    fkv3 = fkv.reshape(db, tqn, 2 * W_FOX)
    lfn = jnp.pad(jnp.swapaxes(misc3[:, :, :H_FOX], 1, 2), ((0, 0), (0, 8 - H_FOX), (0, LANES - tqn)))
    o_fox = _fox_decode(qf.reshape(db, tqn, W_FOX), caches["fox"], caches["logf"], l, page_table,
                        pad_rows(fkv3), lfn, g_pages["fox"])
    skv3 = skv.reshape(db, tqn, 2 * W_STK)
    o_stk = _stick_decode(qs.reshape(db, tqn, W_STK), caches["stk"], l, page_table, pad_rows(skv3), g_pages["stk"])

    cmp3 = cmpr.reshape(db, tqn, 2 * W_GRP)
    kc_p, vc_p = _compress_paged(caches["cmp"], l, page_table, lw, tabs["cos_bs"], tabs["sin_bs"], g_pages["cmp"])
    new_blk = jnp.pad(cmp3, ((0, 0), (0, CMP_BLOCK - tqn), (0, 0))).reshape(1, db * CMP_BLOCK, 2 * W_GRP)
    kc_n, vc_n = _compress_contig(new_blk, lw, tabs["cos_bn"], tabs["sin_bn"])
    nb = past // CMP_BLOCK + 1
    nbp = -(-nb // LANES) * LANES
    cat = lambda a, c: jnp.pad(jnp.concatenate([a, c.reshape(db, 1, LANES)], axis=1), ((0, 0), (0, nbp - nb), (0, 0)))
    kc = cat(kc_p, kc_n)
    vc = cat(vc_p, vc_n)
    win3 = win.reshape(db, tqn, 2 * W_GRP)
    win_all = jnp.concatenate([caches["win"][l], win3], axis=1)
    wk = win_all.shape[1]
    wkp = -(-wk // LANES) * LANES
    q3 = qn.reshape(db, tqn, W_NSA)
    o_c, o_w, selm = _nsa_sample_small(q3, kc, vc, jnp.pad(win_all, ((0, 0), (0, wkp - wk), (0, 0))),
                                       nb, past, past + tqn - wk)
    sel3 = sel.reshape(db, tqn, 2 * W_GRP)
    o_nsa = _sel_decode(q3, caches["sel"], l, page_table, pad_rows(sel3), selm, o_c, o_w, misc3, consts, g_pages["sel"])

    x_out = _mix_ffn(x3.reshape(n, d), o_fox.reshape(n, W_FOX), o_nsa.reshape(n, W_NSA), o_stk.reshape(n, W_STK),
                     lw, n, _pick(lw["w_gate"].shape[1], (1408, 1024, 512, 256, 128)))
    n_win = caches["win"].shape[2]
    rows = (fkv.reshape(db, tqn, 2, H_FOX, HEAD_DIM), misc3[:, :, :H_FOX], cmpr.reshape(db, tqn, 2, G_NSA, HEAD_DIM),
            sel.reshape(db, tqn, 2, G_NSA, HEAD_DIM), win_all[:, wk - n_win:].reshape(db, n_win, 2, G_NSA, HEAD_DIM),
            skv.reshape(db, tqn, 2, H_STK, HEAD_DIM))
    return x_out.reshape(db, tqn, d), rows


def kernel(x_prompt, x_sample, cache_fox_kv, cache_fox_logf, cache_nsa_cmp_kv, cache_nsa_sel_kv, state_nsa_win_kv, cache_stk_kv, page_table, g_attn, w_in, b_fox_f, b_nsa_gate, fox_gq, fox_gk, nsa_gq, nsa_gkc, nsa_gks, nsa_gkw, nsa_pe_k, nsa_w1k, nsa_w2k, nsa_pe_v, nsa_w1v, nsa_w2v, g_mix, w_o, g_ffn, w_gate, w_up, w_down):
    weights = dict(g_attn=g_attn, w_in=w_in, b_fox_f=b_fox_f, b_nsa_gate=b_nsa_gate, fox_gq=fox_gq, fox_gk=fox_gk,
                   nsa_gq=nsa_gq, nsa_gkc=nsa_gkc, nsa_gks=nsa_gks, nsa_gkw=nsa_gkw, nsa_pe_k=nsa_pe_k,
                   nsa_w1k=nsa_w1k, nsa_w2k=nsa_w2k, nsa_pe_v=nsa_pe_v, nsa_w1v=nsa_w1v, nsa_w2v=nsa_w2v,
                   g_mix=g_mix, w_o=w_o, g_ffn=g_ffn, w_gate=w_gate, w_up=w_up, w_down=w_down)
    depth = w_in.shape[0]
    b, t, d = x_prompt.shape
    db, tqn, _ = x_sample.shape
    n_pool, page = cache_fox_kv.shape[1], cache_fox_kv.shape[2]
    n_pages = page_table.shape[1]
    past = n_pages * page
    assert t % LANES == 0 and t // CMP_BLOCK <= HEAD_DIM and tqn == 8 and page == LANES
    n_win_s = state_nsa_win_kv.shape[2]

    caches = {
        "fox": cache_fox_kv.reshape(depth, n_pool, page, 2 * W_FOX),
        "logf": jnp.swapaxes(cache_fox_logf, 2, 3),
        "cmp": cache_nsa_cmp_kv.reshape(depth, n_pool, page, 2 * W_GRP),
        "sel": cache_nsa_sel_kv.reshape(depth, n_pool, page, 2 * W_GRP),
        "stk": cache_stk_kv.reshape(depth, n_pool, page, 2 * W_STK),
        "win": state_nsa_win_kv.reshape(depth, db, n_win_s, 2 * W_GRP),
    }
    g_pages = {k: _pick(n_pages, (16, 8, 4, 2, 1)) for k in ("fox", "stk", "sel")}
    g_pages["cmp"] = _pick(n_pages, (32, 16, 8, 4, 2, 1))
    consts = _constants(t, g_pages["sel"], page)

    nbp_blocks = t // CMP_BLOCK
    nbs_blocks = past // CMP_BLOCK
    bend = lambda n0, cnt: (n0 + jnp.arange(cnt, dtype=jnp.int32)) * CMP_BLOCK + (CMP_BLOCK - 1)
    pos_all = jnp.concatenate([
        jnp.arange(t, dtype=jnp.int32),
        jnp.tile(past + jnp.arange(tqn, dtype=jnp.int32), db),
        bend(0, nbp_blocks), bend(0, nbs_blocks), jnp.tile(bend(nbs_blocks, 1), db),
    ])
    pad = (-pos_all.shape[0]) % 8
    cos_all, sin_all = _rope_tables(jnp.pad(pos_all, (0, pad)))
    offs = np.cumsum([0, t, db * tqn, nbp_blocks, nbs_blocks, db])
    names = ("p", "s", "bp", "bs", "bn")
    tabs = {}
    for i, nm in enumerate(names):
        tabs["cos_" + nm] = cos_all[offs[i]:offs[i + 1]]
        tabs["sin_" + nm] = sin_all[offs[i]:offs[i + 1]]

    xp, xs = x_prompt, x_sample
    rows_p, rows_s = [], []
    for l in range(depth):
        lw = _layer_weights(l, weights)
        xp, rp = _prompt_layer(xp, lw, tabs, consts)
        xs, rs = _sample_layer(xs, l, lw, tabs, consts, caches, page_table, g_pages)
        rows_p.append(rp)
        rows_s.append(rs)
    outs = [xp, xs]
    for i in range(6):
        outs.append(jnp.stack([r[i] for r in rows_p], axis=0))
        outs.append(jnp.stack([r[i] for r in rows_s], axis=0))
    return tuple(outs)
```

```python
import functools
import math

import numpy as np
import jax
import jax.numpy as jnp
from jax import lax
from jax.experimental import pallas as pl
from jax.experimental.pallas import tpu as pltpu

F32 = jnp.float32
BF16 = jnp.bfloat16

HEAD_DIM = 64
H_FOX = 4
H_NSA = 8
G_NSA = 2
R_NSA = H_NSA // G_NSA
H_STK = 4
CMP_BLOCK = 64
CMP_SHIFT = 6
CMP_HIDDEN = 256
TOP_N = 16
WINDOW = 512
FORCE_BONUS = 1.0e4
ROPE_THETA = 10000.0
EPS = 1e-6
LANES = 128
SCALE = HEAD_DIM ** -0.5

W_FOX = H_FOX * HEAD_DIM
W_NSA = H_NSA * HEAD_DIM
W_GRP = G_NSA * HEAD_DIM
W_STK = H_STK * HEAD_DIM
N_MISC = H_FOX + 3 * H_NSA

C_FQ, C_FKV, C_NQ, C_CMP, C_SEL, C_WIN, C_SQ, C_SKV, C_MISC = 0, 256, 768, 1280, 1536, 1792, 2048, 2304, 2816
W_PROJ = 2944

NSA_HEAD_ORDER = (0, 4, 1, 5, 2, 6, 3, 7)

VMEM_LIMIT = 56 * 1024 * 1024


def _cparams(*sem):
    return pltpu.CompilerParams(dimension_semantics=sem, vmem_limit_bytes=VMEM_LIMIT)


def _iota(shape, dim):
    return lax.broadcasted_iota(jnp.int32, shape, dim)


def _log_sigmoid(z):
    return jnp.minimum(z, 0.0) - jnp.log1p(jnp.exp(-jnp.abs(z)))


def _dot(a, b):
    return jnp.dot(a, b, preferred_element_type=F32)


def _dot_nt(a, b):
    return lax.dot_general(a, b, (((1,), (1,)), ((), ())), preferred_element_type=F32)


def _split3(x):
    x1 = x.astype(BF16)
    r1 = x - x1.astype(F32)
    x2 = r1.astype(BF16)
    x3 = (r1 - x2.astype(F32)).astype(BF16)
    return x1, x2, x3


def _dot_exact_rhs(x, e):
    x1, x2, x3 = _split3(x)
    return _dot(x1, e) + _dot(x2, e) + _dot(x3, e)


def _dot_exact_lhs(e, x):
    x1, x2, x3 = _split3(x)
    return _dot(e, x1) + _dot(e, x2) + _dot(e, x3)


def _head_norm(y, s64, gain):
    w = y.shape[1]
    y2 = (y * y).astype(BF16)
    outs = []
    for c in range(0, w, 256):
        cw = min(256, w - c)
        ms = _dot(y2[:, c:c + cw], s64[:cw, :cw])
        outs.append(y[:, c:c + cw] * lax.rsqrt(ms + EPS))
    yn = outs[0] if len(outs) == 1 else jnp.concatenate(outs, axis=1)
    return yn * gain


def _rope(y, cos, sin):
    w = y.shape[1]
    lane = _iota(y.shape, 1)
    first = (lane & 63) < 32
    sw = jnp.where(first, pltpu.roll(y, w - 32, 1), pltpu.roll(y, 32, 1))
    if w > LANES:
        cos = jnp.concatenate([cos] * (w // LANES), axis=1)
        sin = jnp.concatenate([sin] * (w // LANES), axis=1)
    return y * cos + sw * sin


def _suffix_excl_lanes(x):
    lane = _iota(x.shape, 1)
    y = x
    s = 1
    while s < LANES:
        y = y + jnp.where(lane + s < LANES, pltpu.roll(y, LANES - s, 1), 0.0)
        s *= 2
    return y - x, y[:, 0:1]


def _stack_heads(q, n_heads):
    lane = _iota(q.shape, 1)
    zero = jnp.zeros_like(q)
    return jnp.concatenate([jnp.where((lane >> 6) == h, q, zero) for h in range(n_heads)], axis=0)


def _unstack_heads(o, n_heads, t):
    lane = _iota((t, o.shape[1]), 1)
    out = jnp.zeros((t, o.shape[1]), F32)
    for h in range(n_heads):
        out = out + jnp.where((lane >> 6) == h, o[h * t:(h + 1) * t], 0.0)
    return out


def _stack_nsa(q):
    t = q.shape[0]
    lane = _iota((t, LANES), 1)
    blocks = []
    for c in range(R_NSA):
        qc = q[:, c * LANES:(c + 1) * LANES]
        for g in range(G_NSA):
            blocks.append(jnp.where((lane >> 6) == g, qc, jnp.zeros_like(qc)))
    return jnp.concatenate(blocks, axis=0)


def _unstack_nsa(o, t):
    lane = _iota((t, LANES), 1)
    chunks = []
    for c in range(R_NSA):
        a = o[(2 * c) * t:(2 * c + 1) * t]
        b = o[(2 * c + 1) * t:(2 * c + 2) * t]
        chunks.append(jnp.where((lane >> 6) == 0, a, b))
    return jnp.concatenate(chunks, axis=1)


def _softmax_update(s, mask, m_ref, l_ref, acc_ref, vt):
    if mask is not None:
        s = jnp.where(mask, s, -jnp.inf)
    m_old = m_ref[...]
    m_new = jnp.maximum(m_old, jnp.max(s, axis=1, keepdims=True))
    m_safe = jnp.where(m_new > -jnp.inf, m_new, 0.0)
    alpha = jnp.exp(m_old - m_safe)
    p = jnp.exp(s - m_safe)
    l_ref[...] = alpha * l_ref[...] + jnp.sum(p, axis=1, keepdims=True)
    acc_ref[...] = alpha * acc_ref[...] + _dot_nt(p.astype(BF16), vt)
    m_ref[...] = m_new


def _tile_lanes(x, w):
    return x if w == LANES else jnp.concatenate([x] * (w // LANES), axis=1)


def _strip_softmax(s, mask, r0, strip, m_ref, l_ref, a_ref, p_ref, guard):
    rows = pl.ds(r0, strip)
    if mask is not None:
        s = jnp.where(mask, s, -jnp.inf)
    m_old = m_ref[rows, :]
    m_new = jnp.maximum(m_old, jnp.max(s, axis=1, keepdims=True))
    m_use = jnp.where(m_new > -jnp.inf, m_new, 0.0) if guard else m_new
    alpha = jnp.exp(m_old - m_use)
    p = jnp.exp(s - _tile_lanes(m_use, s.shape[1]))
    l_ref[rows, :] = alpha * l_ref[rows, :] + jnp.sum(p, axis=1, keepdims=True)
    m_ref[rows, :] = m_new
    a_ref[rows, :] = alpha
    p_ref[rows, :] = p.astype(BF16)


def _gate_expand(misc, eg_ref):
    m1, m2, m3 = _split3(misc)
    outs = []
    for j in range(3):
        e = eg_ref[j]
        outs.append(_dot(m1, e) + _dot(m2, e) + _dot(m3, e))
    return outs


def _rope_table_kernel(pos_ref, inv_ref, sgn_ref, cos_ref, sin_ref):
    ang = pos_ref[...] * inv_ref[...]
    cos_ref[...] = jnp.cos(ang)
    sin_ref[...] = jnp.sin(ang) * sgn_ref[...]


def _rope_tables(pos):
    p = pos.shape[0]
    half = HEAD_DIM // 2
    inv = ROPE_THETA ** (-jnp.arange(half, dtype=F32) / half)
    inv128 = jnp.tile(inv, 4)[None, :]
    sgn128 = jnp.tile(jnp.concatenate([-jnp.ones((half,), F32), jnp.ones((half,), F32)]), 2)[None, :]
    return pl.pallas_call(
        _rope_table_kernel,
        out_shape=(jax.ShapeDtypeStruct((p, LANES), F32), jax.ShapeDtypeStruct((p, LANES), F32)),
        name="rope_tables",
    )(pos.astype(F32)[:, None], inv128, sgn128)


def _proj_kernel(x_ref, g_ref, w_ref, bias_ref, gains_ref, cos_ref, sin_ref, s64_ref,
                 qf_ref, fkv_ref, qn_ref, cmp_ref, sel_ref, win_ref, qs_ref, skv_ref, misc_ref):
    x = x_ref[...]
    ms = jnp.mean(x * x, axis=-1, keepdims=True)
    xn = (x * lax.rsqrt(ms + EPS) * g_ref[...]).astype(BF16)
    cos = cos_ref[...]
    sin = sin_ref[...]
    s64 = s64_ref[...]

    def mm(c0, w):
        return _dot(xn, w_ref[:, c0:c0 + w])

    qf_ref[...] = (_head_norm(mm(C_FQ, W_FOX), s64, gains_ref[0:1, :W_FOX]) * SCALE).astype(BF16)
    kv = mm(C_FKV, 2 * W_FOX)
    fkv_ref[:, :W_FOX] = _head_norm(kv[:, :W_FOX], s64, gains_ref[1:2, :W_FOX])
    fkv_ref[:, W_FOX:] = kv[:, W_FOX:]
    qn = _rope(_head_norm(mm(C_NQ, W_NSA), s64, gains_ref[2:3, :W_NSA]), cos, sin)
    qn_ref[...] = (qn * SCALE).astype(BF16)
    cmp_ref[...] = mm(C_CMP, 2 * W_GRP)
    sel = mm(C_SEL, 2 * W_GRP)
    sel_ref[:, :W_GRP] = _rope(_head_norm(sel[:, :W_GRP], s64, gains_ref[3:4, :W_GRP]), cos, sin)
    sel_ref[:, W_GRP:] = sel[:, W_GRP:]
    win = mm(C_WIN, 2 * W_GRP)
    win_ref[:, :W_GRP] = _rope(_head_norm(win[:, :W_GRP], s64, gains_ref[4:5, :W_GRP]), cos, sin)
    win_ref[:, W_GRP:] = win[:, W_GRP:]
    qs_ref[...] = (mm(C_SQ, W_STK) * SCALE).astype(BF16)
    skv_ref[...] = mm(C_SKV, 2 * W_STK)
    c = mm(C_MISC, LANES) + bias_ref[...]
    lane = _iota(c.shape, 1)
    misc_ref[...] = jnp.where(lane < H_FOX, _log_sigmoid(c), 1.0 / (1.0 + jnp.exp(-c)))


def _proj(x2d, lw, cos, sin, n_pos_blocks, tn):
    n, d = x2d.shape
    nt = n // tn
    row = lambda i: (i, 0)
    const = lambda i: (0, 0)
    widths = (W_FOX, 2 * W_FOX, W_NSA, 2 * W_GRP, 2 * W_GRP, 2 * W_GRP, W_STK, 2 * W_STK, LANES)
    dtypes = (BF16, F32, BF16, F32, F32, F32, BF16, F32, F32)
    out_shape = tuple(jax.ShapeDtypeStruct((n, w), dt) for w, dt in zip(widths, dtypes))
    out_specs = tuple(pl.BlockSpec((tn, w), row) for w in widths)
    pos_map = lambda i: (i % n_pos_blocks, 0)
    return pl.pallas_call(
        _proj_kernel,
        grid=(nt,),
        in_specs=[
            pl.BlockSpec((tn, d), row),
            pl.BlockSpec((1, d), const),
            pl.BlockSpec((d, W_PROJ), const),
            pl.BlockSpec((1, LANES), const),
            pl.BlockSpec((8, W_NSA), const),
            pl.BlockSpec((tn, LANES), pos_map),
            pl.BlockSpec((tn, LANES), pos_map),
            pl.BlockSpec((256, 256), const),
        ],
        out_specs=out_specs,
        out_shape=out_shape,
        compiler_params=_cparams("arbitrary"),
        name="proj",
    )(x2d, lw["g_attn"], lw["w_in"], lw["bias_misc"], lw["gains"], cos, sin, lw["s64"])


def _cumsum_kernel(misc_ref, ccol_ref, crow_ref, *, blk):
    t = misc_ref.shape[1]
    r = _iota((blk, blk), 0)
    c = _iota((blk, blk), 1)
    tri = jnp.where(r >= c, 1.0, 0.0).astype(BF16)
    carry = jnp.zeros((1, LANES), F32)
    for i in range(t // blk):
        x = misc_ref[0, i * blk:(i + 1) * blk, :]
        cs = _dot_exact_lhs(tri, x) + carry
        carry = cs[blk - 1:blk, :]
        ccol_ref[0, i * blk:(i + 1) * blk, :] = cs
        crow_ref[0, i] = cs.T[0:8, :]


def _cumsum(misc3, blk):
    b, t, _ = misc3.shape
    return pl.pallas_call(
        functools.partial(_cumsum_kernel, blk=blk),
        grid=(b,),
        in_specs=[pl.BlockSpec((1, t, LANES), lambda i: (i, 0, 0))],
        out_specs=(pl.BlockSpec((1, t, LANES), lambda i: (i, 0, 0)),
                   pl.BlockSpec((1, t // blk, 8, blk), lambda i: (i, 0, 0, 0))),
        out_shape=(jax.ShapeDtypeStruct((b, t, LANES), F32),
                   jax.ShapeDtypeStruct((b, t // blk, 8, blk), F32)),
        compiler_params=_cparams("arbitrary"),
        name="logf_cumsum",
    )(misc3)


def _fox_prompt_kernel(q_ref, kv_ref, ccol_ref, crow_ref, o_ref, kb, vb, q4_s, cq_s, s_s, p_s, m_s, l_s, a_s, acc_s,
                       *, tq, strip):
    qi = pl.program_id(1)

    @pl.when(qi == 0)
    def _():
        kb[...] = kv_ref[0, :, :W_FOX].astype(BF16)
        vb[...] = kv_ref[0, :, W_FOX:].astype(BF16)

    q4_s[...] = _stack_heads(q_ref[0], H_FOX)
    cc = ccol_ref[0]
    cq_s[...] = jnp.concatenate([jnp.broadcast_to(cc[:, h:h + 1], (tq, LANES)) for h in range(H_FOX)], axis=0)
    m_s[...] = jnp.full(m_s.shape, -jnp.inf, F32)
    l_s[...] = jnp.zeros(l_s.shape, F32)
    acc_s[...] = jnp.zeros(acc_s.shape, F32)
    q0 = qi * tq

    def tile(kj, diag):
        k0 = pl.multiple_of(kj * tq, tq)
        s_s[...] = _dot_nt(q4_s[...], kb[pl.ds(k0, tq), :])
        cr = crow_ref[0, kj]
        kpos = k0 + _iota((1, tq), 1)
        for h in range(H_FOX):
            ck = cr[h:h + 1, :]

            def strip_body(i, c, h=h, ck=ck):
                r0 = h * tq + i * strip
                s = s_s[pl.ds(r0, strip), :] + _tile_lanes(cq_s[pl.ds(r0, strip), :], tq) - ck
                mask = (kpos <= q0 + i * strip + _iota((strip, 1), 0)) if diag else None
                _strip_softmax(s, mask, r0, strip, m_s, l_s, a_s, p_s, False)
                return c

            for i in range(tq // strip):
                strip_body(i, 0)
        acc_s[...] = _tile_lanes(a_s[...], W_FOX) * acc_s[...] + _dot(p_s[...], vb[pl.ds(k0, tq), :])

    def body(kj, c):
        tile(kj, False)
        return c

    lax.fori_loop(0, qi, body, 0)
    tile(qi, True)
    o4 = acc_s[...] / _tile_lanes(jnp.maximum(l_s[...], 1e-30), W_FOX)
    o_ref[0] = _unstack_heads(o4, H_FOX, tq)


def _fox_prompt(q3, kv3, ccol, crow, tq, tk):
    b, t, _ = q3.shape
    assert tq == tk
    rows = H_FOX * tq
    return pl.pallas_call(
        functools.partial(_fox_prompt_kernel, tq=tq, strip=32),
        grid=(b, t // tq),
        in_specs=[
            pl.BlockSpec((1, tq, W_FOX), lambda i, j: (i, j, 0)),
            pl.BlockSpec((1, t, 2 * W_FOX), lambda i, j: (i, 0, 0)),
            pl.BlockSpec((1, tq, LANES), lambda i, j: (i, j, 0)),
            pl.BlockSpec((1, t // tk, 8, tk), lambda i, j: (i, 0, 0, 0)),
        ],
        out_specs=pl.BlockSpec((1, tq, W_FOX), lambda i, j: (i, j, 0)),
        out_shape=jax.ShapeDtypeStruct((b, t, W_FOX), F32),
        scratch_shapes=[
            pltpu.VMEM((t, W_FOX), BF16), pltpu.VMEM((t, W_FOX), BF16),
            pltpu.VMEM((rows, W_FOX), BF16), pltpu.VMEM((rows, LANES), F32),
            pltpu.VMEM((rows, tq), F32), pltpu.VMEM((rows, tq), BF16),
            pltpu.VMEM((rows, LANES), F32), pltpu.VMEM((rows, LANES), F32), pltpu.VMEM((rows, LANES), F32),
            pltpu.VMEM((rows, W_FOX), F32),
        ],
        compiler_params=_cparams("arbitrary", "arbitrary"),
        name="fox_prompt",
    )(q3, kv3, ccol, crow)


def _stick_prompt_kernel(q_ref, kv_ref, o_ref, kb, vb, tri_s, q4_s, z_s, hi_s, lo_s, rc_s, p_s, rs_s, r_s, acc_s,
                         *, tq, strip):
    qi = pl.program_id(1)

    @pl.when(qi == 0)
    def _():
        kb[...] = kv_ref[0, :, :W_STK].astype(BF16)
        vb[...] = kv_ref[0, :, W_STK:].astype(BF16)
        tri_s[...] = jnp.where(_iota((tq, tq), 0) > _iota((tq, tq), 1), 1.0, 0.0).astype(BF16)

    q4_s[...] = _stack_heads(q_ref[0], H_STK)
    r_s[...] = jnp.zeros(r_s.shape, F32)
    acc_s[...] = jnp.zeros(acc_s.shape, F32)
    q0 = qi * tq
    n_strips = H_STK * tq // strip

    def tile(kj, diag):
        k0 = pl.multiple_of(kj * tq, tq)
        z_s[...] = _dot_nt(q4_s[...], kb[pl.ds(k0, tq), :])
        kpos = k0 + _iota((1, tq), 1)

        def strip_mask(i):
            tq_off = (i * strip) & (tq - 1)
            return kpos < q0 + tq_off + _iota((strip, 1), 0)

        def pass1(i, c):
            rows = pl.ds(i * strip, strip)
            z = z_s[rows, :]
            ls = _log_sigmoid(z)
            lk = ls - z
            if diag:
                lk = jnp.where(strip_mask(i), lk, 0.0)
            hi = lk.astype(BF16)
            hi_s[rows, :] = hi
            lo_s[rows, :] = (lk - hi.astype(F32)).astype(BF16)
            z_s[rows, :] = ls
            rs_s[rows, :] = jnp.sum(lk, axis=1, keepdims=True)
            return c

        for i in range(n_strips):
            pass1(i, 0)
        tri = tri_s[...]
        rc_s[...] = _dot(hi_s[...], tri) + _dot(lo_s[...], tri)

        def pass2(i, c):
            rows = pl.ds(i * strip, strip)
            a = jnp.exp(z_s[rows, :] + rc_s[rows, :] + r_s[rows, :])
            if diag:
                a = jnp.where(strip_mask(i), a, 0.0)
            p_s[rows, :] = a.astype(BF16)
            return c

        for i in range(n_strips):
            pass2(i, 0)
        acc_s[...] = acc_s[...] + _dot(p_s[...], vb[pl.ds(k0, tq), :])
        r_s[...] = r_s[...] + rs_s[...]

    tile(qi, True)

    def body(it, c):
        tile(qi - 1 - it, False)
        return c

    lax.fori_loop(0, qi, body, 0)
    o_ref[0] = _unstack_heads(acc_s[...], H_STK, tq)


def _stick_prompt(q3, kv3, tq, tk):
    b, t, _ = q3.shape
    assert tq == tk and (tq & (tq - 1)) == 0
    rows = H_STK * tq
    return pl.pallas_call(
        functools.partial(_stick_prompt_kernel, tq=tq, strip=32),
        grid=(b, t // tq),
        in_specs=[
            pl.BlockSpec((1, tq, W_STK), lambda i, j: (i, j, 0)),
            pl.BlockSpec((1, t, 2 * W_STK), lambda i, j: (i, 0, 0)),
        ],
        out_specs=pl.BlockSpec((1, tq, W_STK), lambda i, j: (i, j, 0)),
        out_shape=jax.ShapeDtypeStruct((b, t, W_STK), F32),
        scratch_shapes=[
            pltpu.VMEM((t, W_STK), BF16), pltpu.VMEM((t, W_STK), BF16), pltpu.VMEM((tq, tq), BF16),
            pltpu.VMEM((rows, W_STK), BF16),
            pltpu.VMEM((rows, tq), F32), pltpu.VMEM((rows, tq), BF16), pltpu.VMEM((rows, tq), BF16),
            pltpu.VMEM((rows, tq), F32), pltpu.VMEM((rows, tq), BF16),
            pltpu.VMEM((rows, 1), F32), pltpu.VMEM((rows, 1), F32),
            pltpu.VMEM((rows, W_STK), F32),
        ],
        compiler_params=_cparams("arbitrary", "arbitrary"),
        name="stick_prompt",
    )(q3, kv3)


def _compress_body(x_refs, pe_ref, w1k_ref, w1v_ref, w2k_ref, w2v_ref, gkc_ref, cos_ref, sin_ref, s64_ref,
                   kc_ref, vc_ref, xs):
    off = 0
    for r in x_refs:
        rows = r.shape[-2]
        v = r[...].reshape(rows, 2 * W_GRP)
        xs[0, off:off + rows, :] = v[:, :LANES]
        xs[1, off:off + rows, :] = v[:, LANES:]
        off += rows
    nblk = off // CMP_BLOCK
    lane = _iota((nblk, LANES), 1)
    lo_half = lane < HEAD_DIM
    hk = jnp.zeros((2 * nblk, CMP_HIDDEN), F32)
    hv = jnp.zeros((2 * nblk, CMP_HIDDEN), F32)
    for i4 in range(CMP_BLOCK // 4):
        x = [[xs[h, pl.ds(4 * i4 + j, nblk, stride=CMP_BLOCK), :] for h in range(2)] for j in range(4)]
        sw = [[pltpu.roll(v, HEAD_DIM, 1) for v in xj] for xj in x]

        def chunk_at(j, c, pos):
            return x[j][c // 2] if (c % 2) == pos else sw[j][c // 2]

        packed = []
        for c in range(4):
            left = jnp.where(lo_half, chunk_at(0, c, 0), chunk_at(1, c, 1))
            right = jnp.where(lo_half, chunk_at(2, c, 0), chunk_at(3, c, 1))
            packed.append(jnp.concatenate([left, right], axis=1))
        ak = jnp.concatenate([packed[0], packed[1]], axis=0) + pe_ref[0, i4:i4 + 1, :]
        av = jnp.concatenate([packed[2], packed[3]], axis=0) + pe_ref[1, i4:i4 + 1, :]
        hk = hk + _dot(ak.astype(BF16), w1k_ref[256 * i4:256 * (i4 + 1), :])
        hv = hv + _dot(av.astype(BF16), w1v_ref[256 * i4:256 * (i4 + 1), :])
    gk = jax.nn.gelu(hk).astype(BF16)
    gv = jax.nn.gelu(hv).astype(BF16)
    kc = _dot(gk[:nblk], w2k_ref[0]) + _dot(gk[nblk:], w2k_ref[1])
    kc = _rope(_head_norm(kc, s64_ref[...], gkc_ref[...]), cos_ref[...], sin_ref[...])
    kc_ref[...] = kc.reshape(kc_ref.shape)
    vc = _dot(gv[:nblk], w2v_ref[0]) + _dot(gv[nblk:], w2v_ref[1])
    vc_ref[...] = vc.reshape(vc_ref.shape)


def _compress_kernel(*refs, n_x, n_prefetch):
    refs = refs[n_prefetch:]
    _compress_body(refs[:n_x], *refs[n_x:])


def _compress_weight_specs(nmap):
    return [
        pl.BlockSpec((2, CMP_BLOCK // 4, 256), nmap(3)),
        pl.BlockSpec((CMP_BLOCK * HEAD_DIM, CMP_HIDDEN), nmap(2)),
        pl.BlockSpec((CMP_BLOCK * HEAD_DIM, CMP_HIDDEN), nmap(2)),
        pl.BlockSpec((2, CMP_HIDDEN, LANES), nmap(3)),
        pl.BlockSpec((2, CMP_HIDDEN, LANES), nmap(3)),
        pl.BlockSpec((1, LANES), nmap(2)),
    ]


def _compress_contig(x3, lw, cos, sin):
    b, t, _ = x3.shape
    nblk = t // CMP_BLOCK
    nmap = lambda n: (lambda i: (0,) * n)
    return pl.pallas_call(
        functools.partial(_compress_kernel, n_x=1, n_prefetch=0),
        grid=(b,),
        in_specs=[pl.BlockSpec((1, t, 2 * W_GRP), lambda i: (i, 0, 0))] + _compress_weight_specs(nmap) + [
            pl.BlockSpec((nblk, LANES), nmap(2)),
            pl.BlockSpec((nblk, LANES), nmap(2)),
            pl.BlockSpec((256, 256), nmap(2)),
        ],
        out_specs=(pl.BlockSpec((1, nblk, LANES), lambda i: (i, 0, 0)),
                   pl.BlockSpec((1, nblk, LANES), lambda i: (i, 0, 0))),
        out_shape=(jax.ShapeDtypeStruct((b, nblk, LANES), F32), jax.ShapeDtypeStruct((b, nblk, LANES), F32)),
        scratch_shapes=[pltpu.VMEM((2, t, LANES), F32)],
        compiler_params=_cparams("arbitrary"),
        name="nsa_compress",
    )(x3, lw["pe4"], lw["w1k"], lw["w1v"], lw["w2k"], lw["w2v"], lw["gkc"], cos, sin, lw["s64"])


def _block_select(score, visible, lane_n, nb, n_sel, group_lanes):
    cnt = jnp.zeros(score.shape, F32)
    n_slots = score.shape[1] // group_lanes
    lane = _iota(score.shape, 1)
    for j in range(nb):
        if n_slots == 1:
            vj = score[:, j:j + 1]
        else:
            vj = score[:, j:j + 1]
            for sl in range(1, n_slots):
                vj = jnp.where(lane >= sl * group_lanes, score[:, sl * group_lanes + j:sl * group_lanes + j + 1], vj)
        ahead = (vj > score) | ((vj == score) & (lane_n > j))
        cnt = cnt + jnp.where(ahead, 1.0, 0.0)
    return visible & (cnt < n_sel)


def _masked_softmax_rows(s, mask):
    s = jnp.where(mask, s, -jnp.inf)
    m = jnp.max(s, axis=1, keepdims=True)
    m = jnp.where(m > -jnp.inf, m, 0.0)
    e = jnp.where(mask, jnp.exp(s - m), 0.0)
    return e / jnp.maximum(jnp.sum(e, axis=1, keepdims=True), 1e-30)


def _nsa_prompt_kernel(q_ref, kc_ref, vc_ref, sel_ref, win_ref, misc_ref, eg_ref, eb_ref, o_ref,
                       selb, winb, kcs, vcs, qs_s, selm_s, me_s, s_s, p_s, m_s, l_s, a_s, acc_s, sw_s, pw_s,
                       *, tq, tk, nb, span, strip):
    qi = pl.program_id(1)

    @pl.when(qi == 0)
    def _():
        selb[...] = sel_ref[0].astype(BF16)
        winb[...] = win_ref[0].astype(BF16)
        lane = _iota((nb, LANES), 1)
        kc = kc_ref[0]
        vc = vc_ref[0]
        kcs[...] = jnp.zeros(kcs.shape, BF16)
        vcs[...] = jnp.zeros(vcs.shape, BF16)
        for g in range(G_NSA):
            keep = (lane >> 6) == g
            kcs[g * HEAD_DIM:g * HEAD_DIM + nb, :] = jnp.where(keep, kc, 0.0).astype(BF16)
            vcs[g * HEAD_DIM:g * HEAD_DIM + nb, :] = jnp.where(keep, vc, 0.0).astype(BF16)

    q = q_ref[0]
    q0 = qi * tq
    tpos = q0 + _iota((tq, 1), 0)
    lane = _iota((tq, LANES), 1)
    n_idx = lane & 63
    half = lane >> 6

    cmask = ((n_idx << CMP_SHIFT) + (CMP_BLOCK - 1) <= tpos) & (n_idx < nb)
    imp = jnp.zeros((tq, LANES), F32)
    oc_chunks = []
    for c in range(R_NSA):
        s = _dot_nt(q[:, c * LANES:(c + 1) * LANES], kcs[...])
        s = jnp.where(cmask, s, -jnp.inf)
        m0 = jnp.max(jnp.where(half == 0, s, -jnp.inf), axis=1, keepdims=True)
        m1 = jnp.max(jnp.where(half == 1, s, -jnp.inf), axis=1, keepdims=True)
        m = jnp.where(half == 0, m0, m1)
        m = jnp.where(m > -jnp.inf, m, 0.0)
        e = jnp.where(cmask, jnp.exp(s - m), 0.0)
        d0 = jnp.sum(jnp.where(half == 0, e, 0.0), axis=1, keepdims=True)
        d1 = jnp.sum(jnp.where(half == 1, e, 0.0), axis=1, keepdims=True)
        pc = e / jnp.maximum(jnp.where(half == 0, d0, d1), 1e-30)
        imp = imp + pc
        oc_chunks.append(_dot(pc.astype(BF16), vcs[...]))
    o_c = jnp.concatenate(oc_chunks, axis=1)

    visible = ((n_idx << CMP_SHIFT) <= tpos) & (n_idx < nb)
    cur = tpos >> CMP_SHIFT
    forced = (n_idx == 0) | (n_idx == cur) | (n_idx == cur - 1)
    score = jnp.where(visible, imp + FORCE_BONUS * jnp.where(forced, 1.0, 0.0), -jnp.inf)
    selected = _block_select(score, visible, n_idx, nb, min(TOP_N, nb), HEAD_DIM)
    selm_s[...] = jnp.where(selected, 1.0, 0.0).astype(BF16)

    qs_s[...] = _stack_nsa(q)
    n_blocks = G_NSA * R_NSA

    m_s[...] = jnp.full(m_s.shape, -jnp.inf, F32)
    l_s[...] = jnp.zeros(l_s.shape, F32)
    acc_s[...] = jnp.zeros(acc_s.shape, F32)

    def sel_tile(kj, diag):
        k0 = pl.multiple_of(kj * tk, tk)
        s_s[...] = _dot_nt(qs_s[...], selb[pl.ds(k0, tk), :W_GRP])
        for g in range(G_NSA):
            me_s[g] = _dot(selm_s[...], eb_ref[g, kj])
        kpos = k0 + _iota((1, tk), 1)
        for blk in range(n_blocks):

            def strip_body(i, c, blk=blk):
                off = i * strip
                r0 = blk * tq + i * strip
                mk = me_s[blk % G_NSA, pl.ds(off, strip), :] > 0.5
                if diag:
                    mk = mk & (kpos <= q0 + i * strip + _iota((strip, 1), 0))
                _strip_softmax(s_s[pl.ds(r0, strip), :], mk, r0, strip, m_s, l_s, a_s, p_s, True)
                return c

            for i in range(tq // strip):
                strip_body(i, 0)
        acc_s[...] = a_s[...] * acc_s[...] + _dot(p_s[...], selb[pl.ds(k0, tk), W_GRP:])

    def body(kj, c):
        sel_tile(kj, False)
        return c

    nk = (q0 + tq - 1) // tk + 1
    lax.fori_loop(0, nk - 1, body, 0)
    sel_tile(nk - 1, True)
    o_s = _unstack_nsa(acc_s[...] / jnp.maximum(l_s[...], 1e-30), tq)

    start = pl.multiple_of(jnp.maximum(q0 + tq - span, 0), LANES)
    sw_s[...] = _dot_nt(qs_s[...], winb[pl.ds(start, span), :W_GRP])
    kposw = start + _iota((1, span), 1)

    def wstrip(i, c):
        rows = pl.ds(i * strip, strip)
        tp = q0 + ((i * strip) & (tq - 1)) + _iota((strip, 1), 0)
        pw = _masked_softmax_rows(sw_s[rows, :], (kposw <= tp) & (kposw > tp - WINDOW))
        pw_s[rows, :] = pw.astype(BF16)
        return c

    for i in range(n_blocks * tq // strip):
        wstrip(i, 0)
    o_w = _unstack_nsa(_dot(pw_s[...], winb[pl.ds(start, span), W_GRP:]), tq)

    g_c, g_s, g_w = _gate_expand(misc_ref[0], eg_ref)
    o_ref[0] = g_c * o_c + g_s * o_s + g_w * o_w


def _nsa_prompt(q3, kc, vc, sel3, win3, misc3, consts, tq, tk):
    b, t, _ = q3.shape
    nb = t // CMP_BLOCK
    span = min(WINDOW + tq, t)
    rows = H_NSA * tq
    assert (tq & (tq - 1)) == 0
    return pl.pallas_call(
        functools.partial(_nsa_prompt_kernel, tq=tq, tk=tk, nb=nb, span=span, strip=32),
        grid=(b, t // tq),
        in_specs=[
            pl.BlockSpec((1, tq, W_NSA), lambda i, j: (i, j, 0)),
            pl.BlockSpec((1, nb, LANES), lambda i, j: (i, 0, 0)),
            pl.BlockSpec((1, nb, LANES), lambda i, j: (i, 0, 0)),
            pl.BlockSpec((1, t, 2 * W_GRP), lambda i, j: (i, 0, 0)),
            pl.BlockSpec((1, t, 2 * W_GRP), lambda i, j: (i, 0, 0)),
            pl.BlockSpec((1, tq, LANES), lambda i, j: (i, j, 0)),
            pl.BlockSpec((3, LANES, W_NSA), lambda i, j: (0, 0, 0)),
            pl.BlockSpec((G_NSA, t // tk, LANES, tk), lambda i, j: (0, 0, 0, 0)),
        ],
        out_specs=pl.BlockSpec((1, tq, W_NSA), lambda i, j: (i, j, 0)),
        out_shape=jax.ShapeDtypeStruct((b, t, W_NSA), F32),
        scratch_shapes=[
            pltpu.VMEM((t, 2 * W_GRP), BF16), pltpu.VMEM((t, 2 * W_GRP), BF16),
            pltpu.VMEM((LANES, LANES), BF16), pltpu.VMEM((LANES, LANES), BF16),
            pltpu.VMEM((rows, LANES), BF16), pltpu.VMEM((tq, LANES), BF16),
            pltpu.VMEM((G_NSA, tq, tk), F32),
            pltpu.VMEM((rows, tk), F32), pltpu.VMEM((rows, tk), BF16),
            pltpu.VMEM((rows, LANES), F32), pltpu.VMEM((rows, LANES), F32), pltpu.VMEM((rows, LANES), F32),
            pltpu.VMEM((rows, LANES), F32),
            pltpu.VMEM((rows, span), F32), pltpu.VMEM((rows, span), BF16),
        ],
        compiler_params=_cparams("arbitrary", "arbitrary"),
        name="nsa_prompt",
    )(q3, kc, vc, sel3, win3, misc3, consts["egate"], consts["eb_prompt"])


def _compress_paged_kernel(pt_ref, *refs, g_pages):
    x_refs = refs[:g_pages]
    (pe_ref, w1k_ref, w1v_ref, w2k_ref, w2v_ref, gkc_ref, cos_ref, sin_ref, s64_ref, kc_ref, vc_ref, xs) = refs[g_pages:]
    gd = g_pages * HEAD_DIM
    for i, r in enumerate(x_refs):
        for kv in range(2):
            for g in range(G_NSA):
                xs[kv, g * gd + i * HEAD_DIM:g * gd + (i + 1) * HEAD_DIM, :] = r[kv, g * HEAD_DIM:(g + 1) * HEAD_DIM, :]
    n_sl = G_NSA * g_pages
    lo = _iota((n_sl, LANES), 1) < CMP_BLOCK
    hs = [jnp.zeros((2 * n_sl, CMP_HIDDEN), F32), jnp.zeros((2 * n_sl, CMP_HIDDEN), F32)]
    w1 = (w1k_ref, w1v_ref)
    for q4 in range(HEAD_DIM // 4):
        for kv in range(2):
            x = [xs[kv, pl.ds(4 * q4 + j, n_sl, stride=HEAD_DIM), :] for j in range(4)]
            sw = [pltpu.roll(v, CMP_BLOCK, 1) for v in x]
            a0 = jnp.concatenate([jnp.where(lo, x[0], sw[1]), jnp.where(lo, x[2], sw[3])], axis=1)
            a1 = jnp.concatenate([jnp.where(lo, sw[0], x[1]), jnp.where(lo, sw[2], x[3])], axis=1)
            a = jnp.concatenate([a0, a1], axis=0) + pe_ref[kv, q4:q4 + 1, :]
            hs[kv] = hs[kv] + _dot(a.astype(BF16), w1[kv][256 * q4:256 * (q4 + 1), :])
    gk = jax.nn.gelu(hs[0]).astype(BF16)
    gv = jax.nn.gelu(hs[1]).astype(BF16)

    def second(gx, w2_ref):
        outs = []
        for par in range(2):
            r0 = par * n_sl
            outs.append(_dot(gx[r0:r0 + g_pages], w2_ref[0]) + _dot(gx[r0 + g_pages:r0 + n_sl], w2_ref[1]))
        return jnp.concatenate(outs, axis=0)

    kc = _rope(_head_norm(second(gk, w2k_ref), s64_ref[...], gkc_ref[...]), cos_ref[...], sin_ref[...])
    vc = second(gv, w2v_ref)
    for par in range(2):
        kc_ref[0, 0, par] = kc[par * g_pages:(par + 1) * g_pages]
        vc_ref[0, 0, par] = vc[par * g_pages:(par + 1) * g_pages]


def _compress_paged(cache5, layer, page_table, lw, cos, sin, g_pages):
    db, n_pages = page_table.shape
    page = cache5.shape[4]
    ns = n_pages // g_pages
    nmap = lambda n: (lambda b, s, pt: (0,) * n)

    def page_spec(i):
        return pl.BlockSpec((None, None, 2, W_GRP, page),
                            lambda b, s, pt, i=i: (layer, pt[b, s * g_pages + i], 0, 0, 0))

    out_spec = pl.BlockSpec((1, 1, 2, g_pages, LANES), lambda b, s, pt: (b, s, 0, 0, 0))
    grid_spec = pltpu.PrefetchScalarGridSpec(
        num_scalar_prefetch=1,
        grid=(db, ns),
        in_specs=[page_spec(i) for i in range(g_pages)] + _compress_weight_specs(nmap) + [
            pl.BlockSpec((2 * g_pages, LANES), lambda b, s, pt: (s, 0)),
            pl.BlockSpec((2 * g_pages, LANES), lambda b, s, pt: (s, 0)),
            pl.BlockSpec((256, 256), nmap(2)),
        ],
        out_specs=(out_spec, out_spec),
        scratch_shapes=[pltpu.VMEM((2, g_pages * W_GRP, page), F32)],
    )
    shp = jax.ShapeDtypeStruct((db, ns, 2, g_pages, LANES), F32)
    kc, vc = pl.pallas_call(
        functools.partial(_compress_paged_kernel, g_pages=g_pages),
        grid_spec=grid_spec,
        out_shape=(shp, shp),
        compiler_params=_cparams("arbitrary", "arbitrary"),
        name="nsa_compress_paged",
    )(page_table, *([cache5] * g_pages), lw["pe4t"], lw["w1kt"], lw["w1vt"], lw["w2k"], lw["w2v"], lw["gkc"],
      cos, sin, lw["s64"])
    nb_past = n_pages * page // CMP_BLOCK
    order = lambda a: a.transpose(0, 1, 3, 2, 4).reshape(db, nb_past, LANES)
    return order(kc), order(vc)


def _fox_decode_kernel(pt_ref, *refs, g_pages):
    kv_refs = refs[:g_pages]
    lf_refs = refs[g_pages:2 * g_pages]
    q_ref, knew_ref, lfn_ref, o_ref, m_s, l_s, acc_s, carry_s, nc_s = refs[2 * g_pages:]
    step = pl.program_id(1)
    tqn = q_ref.shape[1]
    rows = H_FOX * tqn
    q4 = _stack_heads(q_ref[0], H_FOX)
    tqv = _iota((rows, 1), 0) & (tqn - 1)

    def expand_heads(x):
        return jnp.concatenate([jnp.broadcast_to(x[h:h + 1, :], (tqn, x.shape[1])) for h in range(H_FOX)], axis=0)

    @pl.when(step == 0)
    def _():
        lfn = lfn_ref[0]
        lane8 = _iota(lfn.shape, 1)
        cum = lfn
        s = 1
        while s < tqn:
            cum = cum + jnp.where(lane8 >= s, pltpu.roll(cum, s, 1), 0.0)
            s *= 2
        e = expand_heads(cum)
        lane = _iota(e.shape, 1)
        nc = jnp.sum(jnp.where(lane == tqv, e, 0.0), axis=1, keepdims=True)
        nc_s[...] = nc
        carry_s[...] = jnp.zeros(carry_s.shape, F32)
        m_s[...] = jnp.full(m_s.shape, -jnp.inf, F32)
        l_s[...] = jnp.zeros(l_s.shape, F32)
        acc_s[...] = jnp.zeros(acc_s.shape, F32)
        sc = _dot(q4, knew_ref[0, 0].astype(BF16)) + nc - e
        _softmax_update(sc, lane <= tqv, m_s, l_s, acc_s, knew_ref[0, 1].astype(BF16))

    kt = jnp.concatenate([r[0].astype(BF16) for r in kv_refs], axis=1)
    vt = jnp.concatenate([r[1].astype(BF16) for r in kv_refs], axis=1)
    sc = _dot(q4, kt)
    carry = carry_s[...]
    xs = []
    for r in lf_refs:
        excl, tot = _suffix_excl_lanes(r[...])
        xs.append(excl + carry)
        carry = carry + tot
    carry_s[...] = carry
    bias = expand_heads(jnp.concatenate(xs, axis=1))
    _softmax_update(sc + nc_s[...] + bias, None, m_s, l_s, acc_s, vt)

    @pl.when(step == pl.num_programs(1) - 1)
    def _():
        o_ref[0] = _unstack_heads(acc_s[...] / jnp.maximum(l_s[...], 1e-30), H_FOX, tqn)


def _fox_decode(q3, cache5, logf4, layer, page_table, knew, lfn, g_pages):
    db, n_pages = page_table.shape
    page = cache5.shape[4]
    tqn = q3.shape[1]
    ns = n_pages // g_pages
    rows = H_FOX * tqn

    def pg(b, s, pt, i):
        return pt[b, n_pages - 1 - (s * g_pages + i)]

    kv_specs = [pl.BlockSpec((None, None, 2, W_FOX, page), lambda b, s, pt, i=i: (layer, pg(b, s, pt, i), 0, 0, 0))
                for i in range(g_pages)]
    lf_specs = [pl.BlockSpec((None, None, H_FOX, page), lambda b, s, pt, i=i: (layer, pg(b, s, pt, i), 0, 0))
                for i in range(g_pages)]
    grid_spec = pltpu.PrefetchScalarGridSpec(
        num_scalar_prefetch=1,
        grid=(db, ns),
        in_specs=kv_specs + lf_specs + [
            pl.BlockSpec((1, tqn, W_FOX), lambda b, s, pt: (b, 0, 0)),
            pl.BlockSpec((1, 2, W_FOX, page), lambda b, s, pt: (b, 0, 0, 0)),
            pl.BlockSpec((1, 8, LANES), lambda b, s, pt: (b, 0, 0)),
        ],
        out_specs=pl.BlockSpec((1, tqn, W_FOX), lambda b, s, pt: (b, 0, 0)),
        scratch_shapes=[
            pltpu.VMEM((rows, 1), F32), pltpu.VMEM((rows, 1), F32), pltpu.VMEM((rows, W_FOX), F32),
            pltpu.VMEM((H_FOX, 1), F32), pltpu.VMEM((rows, 1), F32),
        ],
    )
    return pl.pallas_call(
        functools.partial(_fox_decode_kernel, g_pages=g_pages),
        grid_spec=grid_spec,
        out_shape=jax.ShapeDtypeStruct((db, tqn, W_FOX), F32),
        compiler_params=_cparams("arbitrary", "arbitrary"),
        name="fox_decode",
    )(page_table, *([cache5] * g_pages), *([logf4] * g_pages), q3, knew, lfn)


def _stick_decode_kernel(pt_ref, *refs, g_pages):
    kv_refs = refs[:g_pages]
    q_ref, knew_ref, o_ref, r_s, acc_s = refs[g_pages:]
    step = pl.program_id(1)
    tqn = q_ref.shape[1]
    rows = H_STK * tqn
    q4 = _stack_heads(q_ref[0], H_STK)
    tqv = _iota((rows, 1), 0) & (tqn - 1)

    def sweep(z, mask, r_col):
        ls = _log_sigmoid(z)
        lk = ls - z
        if mask is not None:
            lk = jnp.where(mask, lk, 0.0)
        rcs = []
        for i in range(z.shape[1] // LANES):
            excl, tot = _suffix_excl_lanes(lk[:, i * LANES:(i + 1) * LANES])
            rcs.append(excl + r_col)
            r_col = r_col + tot
        rc = rcs[0] if len(rcs) == 1 else jnp.concatenate(rcs, axis=1)
        a = jnp.exp(ls + rc)
        if mask is not None:
            a = jnp.where(mask, a, 0.0)
        return a, r_col

    @pl.when(step == 0)
    def _():
        z = _dot(q4, knew_ref[0, 0].astype(BF16))
        lane = _iota(z.shape, 1)
        a, r_col = sweep(z, lane < tqv, jnp.zeros((rows, 1), F32))
        r_s[...] = r_col
        acc_s[...] = _dot_nt(a.astype(BF16), knew_ref[0, 1].astype(BF16))

    kt = jnp.concatenate([r[0].astype(BF16) for r in kv_refs], axis=1)
    vt = jnp.concatenate([r[1].astype(BF16) for r in kv_refs], axis=1)
    a, r_col = sweep(_dot(q4, kt), None, r_s[...])
    r_s[...] = r_col
    acc_s[...] = acc_s[...] + _dot_nt(a.astype(BF16), vt)

    @pl.when(step == pl.num_programs(1) - 1)
    def _():
        o_ref[0] = _unstack_heads(acc_s[...], H_STK, tqn)


def _stick_decode(q3, cache5, layer, page_table, knew, g_pages):
    db, n_pages = page_table.shape
    page = cache5.shape[4]
    tqn = q3.shape[1]
    ns = n_pages // g_pages
    rows = H_STK * tqn
    kv_specs = [pl.BlockSpec((None, None, 2, W_STK, page),
                             lambda b, s, pt, i=i: (layer, pt[b, n_pages - 1 - (s * g_pages + i)], 0, 0, 0))
                for i in range(g_pages)]
    grid_spec = pltpu.PrefetchScalarGridSpec(
        num_scalar_prefetch=1,
        grid=(db, ns),
        in_specs=kv_specs + [
            pl.BlockSpec((1, tqn, W_STK), lambda b, s, pt: (b, 0, 0)),
            pl.BlockSpec((1, 2, W_STK, page), lambda b, s, pt: (b, 0, 0, 0)),
        ],
        out_specs=pl.BlockSpec((1, tqn, W_STK), lambda b, s, pt: (b, 0, 0)),
        scratch_shapes=[pltpu.VMEM((rows, 1), F32), pltpu.VMEM((rows, W_STK), F32)],
    )
    return pl.pallas_call(
        functools.partial(_stick_decode_kernel, g_pages=g_pages),
        grid_spec=grid_spec,
        out_shape=jax.ShapeDtypeStruct((db, tqn, W_STK), F32),
        compiler_params=_cparams("arbitrary", "arbitrary"),
        name="stick_decode",
    )(page_table, *([cache5] * g_pages), q3, knew)


def _nsa_sample_small_kernel(q_ref, kc_ref, vc_ref, win_ref, oc_ref, ow_ref, selm_ref, *, nb, pos0, win_pos0):
    q = q_ref[0]
    tqn = q.shape[0]
    nbp = kc_ref.shape[1]
    qs = _stack_nsa(q)
    rows = qs.shape[0]
    tpos = pos0 + (_iota((rows, 1), 0) & (tqn - 1))
    kc = kc_ref[0].astype(BF16)
    vc = vc_ref[0].astype(BF16)
    n_idx = _iota((rows, nbp), 1)
    cmask = ((n_idx << CMP_SHIFT) + (CMP_BLOCK - 1) <= tpos) & (n_idx < nb)
    pc = _masked_softmax_rows(_dot_nt(qs, kc), cmask)
    oc_ref[0] = _unstack_nsa(_dot(pc.astype(BF16), vc), tqn)

    imp = []
    for g in range(G_NSA):
        acc = jnp.zeros((tqn, nbp), F32)
        for c in range(R_NSA):
            blk = 2 * c + g
            acc = acc + pc[blk * tqn:(blk + 1) * tqn]
        imp.append(acc)
    imp = jnp.concatenate(imp, axis=0)
    tp2 = pos0 + (_iota((G_NSA * tqn, 1), 0) & (tqn - 1))
    n2 = _iota(imp.shape, 1)
    visible = ((n2 << CMP_SHIFT) <= tp2) & (n2 < nb)
    cur = tp2 >> CMP_SHIFT
    forced = (n2 == 0) | (n2 == cur) | (n2 == cur - 1)
    score = jnp.where(visible, imp + FORCE_BONUS * jnp.where(forced, 1.0, 0.0), -jnp.inf)
    selected = _block_select(score, visible, n2, nb, min(TOP_N, nb), nbp)
    selm_ref[0] = jnp.where(selected, 1.0, 0.0)

    kw = win_ref[0, :, :W_GRP].astype(BF16)
    vw = win_ref[0, :, W_GRP:].astype(BF16)
    kpos = win_pos0 + _iota((1, kw.shape[0]), 1)
    pw = _masked_softmax_rows(_dot_nt(qs, kw), (kpos <= tpos) & (kpos > tpos - WINDOW) & (kpos >= win_pos0))
    ow_ref[0] = _unstack_nsa(_dot(pw.astype(BF16), vw), tqn)


def _nsa_sample_small(q3, kc, vc, win, nb, pos0, win_pos0):
    db, tqn, _ = q3.shape
    nbp = kc.shape[1]
    wk = win.shape[1]
    return pl.pallas_call(
        functools.partial(_nsa_sample_small_kernel, nb=nb, pos0=pos0, win_pos0=win_pos0),
        grid=(db,),
        in_specs=[
            pl.BlockSpec((1, tqn, W_NSA), lambda b: (b, 0, 0)),
            pl.BlockSpec((1, nbp, LANES), lambda b: (b, 0, 0)),
            pl.BlockSpec((1, nbp, LANES), lambda b: (b, 0, 0)),
            pl.BlockSpec((1, wk, 2 * W_GRP), lambda b: (b, 0, 0)),
        ],
        out_specs=(pl.BlockSpec((1, tqn, W_NSA), lambda b: (b, 0, 0)),
                   pl.BlockSpec((1, tqn, W_NSA), lambda b: (b, 0, 0)),
                   pl.BlockSpec((1, G_NSA * tqn, nbp), lambda b: (b, 0, 0))),
        out_shape=(jax.ShapeDtypeStruct((db, tqn, W_NSA), F32), jax.ShapeDtypeStruct((db, tqn, W_NSA), F32),
                   jax.ShapeDtypeStruct((db, G_NSA * tqn, nbp), F32)),
        compiler_params=_cparams("arbitrary"),
        name="nsa_sample_small",
    )(q3, kc, vc, win)


def _sel_decode_kernel(pt_ref, *refs, g_pages, nb_new):
    kv_refs = refs[:g_pages]
    (q_ref, knew_ref, msel_ref, mnew_ref, ee_ref, oc_ref, ow_ref, misc_ref, eg_ref,
     o_ref, m_s, l_s, acc_s) = refs[g_pages:]
    step = pl.program_id(1)
    tqn = q_ref.shape[1]
    qs = _stack_nsa(q_ref[0])
    rows = qs.shape[0]
    tqv = _iota((rows, 1), 0) & (tqn - 1)

    @pl.when(step == 0)
    def _():
        m_s[...] = jnp.full(m_s.shape, -jnp.inf, F32)
        l_s[...] = jnp.zeros(l_s.shape, F32)
        acc_s[...] = jnp.zeros(acc_s.shape, F32)
        s = _dot(qs, knew_ref[0, 0].astype(BF16))
        lane = _iota(s.shape, 1)
        mn = mnew_ref[0][:, nb_new:nb_new + 1] > 0.5
        mn = jnp.concatenate([mn] * R_NSA, axis=0)
        _softmax_update(s, mn & (lane <= tqv), m_s, l_s, acc_s, knew_ref[0, 1].astype(BF16))

    kt = jnp.concatenate([r[0].astype(BF16) for r in kv_refs], axis=1)
    vt = jnp.concatenate([r[1].astype(BF16) for r in kv_refs], axis=1)
    s = _dot(qs, kt)
    me = _dot(msel_ref[0, 0].astype(BF16), ee_ref[...]) > 0.5
    _softmax_update(s, jnp.concatenate([me] * R_NSA, axis=0), m_s, l_s, acc_s, vt)

    @pl.when(step == pl.num_programs(1) - 1)
    def _():
        o_s = _unstack_nsa(acc_s[...] / jnp.maximum(l_s[...], 1e-30), tqn)
        g_c, g_s, g_w = _gate_expand(misc_ref[0], eg_ref)
        o_ref[0] = g_c * oc_ref[0] + g_s * o_s + g_w * ow_ref[0]


def _sel_decode(q3, cache5, layer, page_table, knew, selm, o_c, o_w, misc3, consts, g_pages):
    db, n_pages = page_table.shape
    page = cache5.shape[4]
    tqn = q3.shape[1]
    ns = n_pages // g_pages
    rows = H_NSA * tqn
    nbp = selm.shape[2]
    bps = g_pages * page // CMP_BLOCK
    nb_past = n_pages * page // CMP_BLOCK
    msel = selm[:, :, :nb_past].reshape(db, G_NSA * tqn, ns, bps).transpose(0, 2, 1, 3)
    kv_specs = [pl.BlockSpec((None, None, 2, W_GRP, page),
                             lambda b, s, pt, i=i: (layer, pt[b, s * g_pages + i], 0, 0, 0))
                for i in range(g_pages)]
    per_b = lambda b, s, pt: (b, 0, 0)
    grid_spec = pltpu.PrefetchScalarGridSpec(
        num_scalar_prefetch=1,
        grid=(db, ns),
        in_specs=kv_specs + [
            pl.BlockSpec((1, tqn, W_NSA), per_b),
            pl.BlockSpec((1, 2, W_GRP, page), lambda b, s, pt: (b, 0, 0, 0)),
            pl.BlockSpec((1, 1, G_NSA * tqn, bps), lambda b, s, pt: (b, s, 0, 0)),
            pl.BlockSpec((1, G_NSA * tqn, nbp), per_b),
            pl.BlockSpec((bps, g_pages * page), lambda b, s, pt: (0, 0)),
            pl.BlockSpec((1, tqn, W_NSA), per_b),
            pl.BlockSpec((1, tqn, W_NSA), per_b),
            pl.BlockSpec((1, tqn, LANES), per_b),
            pl.BlockSpec((3, LANES, W_NSA), lambda b, s, pt: (0, 0, 0)),
        ],
        out_specs=pl.BlockSpec((1, tqn, W_NSA), per_b),
        scratch_shapes=[pltpu.VMEM((rows, 1), F32), pltpu.VMEM((rows, 1), F32), pltpu.VMEM((rows, LANES), F32)],
    )
    return pl.pallas_call(
        functools.partial(_sel_decode_kernel, g_pages=g_pages, nb_new=nb_past),
        grid_spec=grid_spec,
        out_shape=jax.ShapeDtypeStruct((db, tqn, W_NSA), F32),
        compiler_params=_cparams("arbitrary", "arbitrary"),
        name="nsa_sel_decode",
    )(page_table, *([cache5] * g_pages), q3, knew, msel, selm, consts["ee_decode"], o_c, o_w, misc3,
      consts["egate"])


def _mix_ffn_kernel(x_ref, of_ref, on_ref, os_ref, gmix_ref, wo_ref, gffn_ref, wg_ref, wu_ref, wd_ref, o_ref,
                    x1_s, xn_s, acc_s):
    j = pl.program_id(1)

    def rms(y, g):
        return y * lax.rsqrt(jnp.mean(y * y, axis=-1, keepdims=True) + EPS) * g

    @pl.when(j == 0)
    def _():
        o = jnp.concatenate([
            rms(of_ref[...], gmix_ref[:, :W_FOX]),
            rms(on_ref[...], gmix_ref[:, W_FOX:W_FOX + W_NSA]),
            rms(os_ref[...], gmix_ref[:, W_FOX + W_NSA:]),
        ], axis=1).astype(BF16)
        x1 = x_ref[...] + _dot(o, wo_ref[...])
        x1_s[...] = x1
        xn_s[...] = rms(x1, gffn_ref[...]).astype(BF16)
        acc_s[...] = jnp.zeros(acc_s.shape, F32)

    xn = xn_s[...]
    hg = _dot(xn, wg_ref[...])
    hu = _dot(xn, wu_ref[...])
    h = (hg * (1.0 / (1.0 + jnp.exp(-hg))) * hu).astype(BF16)
    acc_s[...] = acc_s[...] + _dot(h, wd_ref[...])

    @pl.when(j == pl.num_programs(1) - 1)
    def _():
        o_ref[...] = x1_s[...] + acc_s[...]


def _mix_ffn(x2d, o_fox, o_nsa, o_stk, lw, tn, tf):
    n, d = x2d.shape
    dff = lw["w_gate"].shape[1]
    row = lambda i, j: (i, 0)
    const = lambda i, j: (0, 0)
    return pl.pallas_call(
        _mix_ffn_kernel,
        grid=(n // tn, dff // tf),
        in_specs=[
            pl.BlockSpec((tn, d), row),
            pl.BlockSpec((tn, W_FOX), row),
            pl.BlockSpec((tn, W_NSA), row),
            pl.BlockSpec((tn, W_STK), row),
            pl.BlockSpec((1, d), const),
            pl.BlockSpec((d, d), const),
            pl.BlockSpec((1, d), const),
            pl.BlockSpec((d, tf), lambda i, j: (0, j)),
            pl.BlockSpec((d, tf), lambda i, j: (0, j)),
            pl.BlockSpec((tf, d), lambda i, j: (j, 0)),
        ],
        out_specs=pl.BlockSpec((tn, d), row),
        out_shape=jax.ShapeDtypeStruct((n, d), F32),
        scratch_shapes=[pltpu.VMEM((tn, d), F32), pltpu.VMEM((tn, d), BF16), pltpu.VMEM((tn, d), F32)],
        compiler_params=_cparams("arbitrary", "arbitrary"),
        name="mix_ffn",
    )(x2d, o_fox, o_nsa, o_stk, lw["g_mix"], lw["w_o"], lw["g_ffn"], lw["w_gate"], lw["w_up"], lw["w_down"])


def _nsa_lane_perm():
    perm = []
    for h in NSA_HEAD_ORDER:
        perm.extend(range(h * HEAD_DIM, (h + 1) * HEAD_DIM))
    return np.asarray(perm, np.int32)


def _layer_weights(l, w):
    d = w["w_in"].shape[1]
    perm = _nsa_lane_perm()
    win = w["w_in"][l]
    o_fq, o_fk, o_ff, o_nq = 0, W_FOX, 3 * W_FOX, 3 * W_FOX + H_FOX
    o_kc = o_nq + W_NSA
    o_ng = o_kc + 6 * W_GRP
    o_sq = o_ng + 3 * H_NSA
    nq = win[:, o_nq:o_nq + W_NSA][:, perm]
    w_perm = jnp.concatenate([
        win[:, o_fq:o_fq + W_FOX], win[:, o_fk:o_fk + 2 * W_FOX], nq, win[:, o_kc:o_kc + 6 * W_GRP],
        win[:, o_sq:o_sq + W_STK], win[:, o_sq + W_STK:o_sq + 3 * W_STK],
        win[:, o_ff:o_ff + H_FOX], win[:, o_ng:o_ng + 3 * H_NSA], jnp.zeros((d, LANES - N_MISC), F32),
    ], axis=1).astype(BF16)
    bias = jnp.concatenate([w["b_fox_f"][l], w["b_nsa_gate"][l], jnp.zeros((LANES - N_MISC,), F32)])[None, :]

    def tiled(g, width):
        return jnp.pad(jnp.tile(g, width // HEAD_DIM), (0, W_NSA - width))

    gains = jnp.stack([tiled(w["fox_gq"][l], W_FOX), tiled(w["fox_gk"][l], W_FOX), tiled(w["nsa_gq"][l], W_NSA),
                       tiled(w["nsa_gks"][l], W_GRP), tiled(w["nsa_gkw"][l], W_GRP)]
                      + [jnp.zeros((W_NSA,), F32)] * 3)
    s64 = jnp.asarray(np.kron(np.eye(4, dtype=np.float32), np.full((64, 64), 1.0 / 64, np.float32)), BF16)

    def w2pad(w2):
        z = jnp.zeros_like(w2)
        return jnp.stack([jnp.concatenate([w2, z], axis=1), jnp.concatenate([z, w2], axis=1)]).astype(BF16)

    def by_feature(w1):
        return w1.reshape(CMP_BLOCK, HEAD_DIM, CMP_HIDDEN).transpose(1, 0, 2).reshape(CMP_BLOCK * HEAD_DIM, CMP_HIDDEN).astype(BF16)

    g_mix = w["g_mix"][l]
    g_mix = jnp.concatenate([g_mix[:W_FOX], g_mix[W_FOX:W_FOX + W_NSA][perm], g_mix[W_FOX + W_NSA:]])[None, :]
    w_o = w["w_o"][l]
    w_o = jnp.concatenate([w_o[:W_FOX], w_o[W_FOX:W_FOX + W_NSA][perm], w_o[W_FOX + W_NSA:]], axis=0).astype(BF16)
    return {
        "g_attn": w["g_attn"][l][None, :], "w_in": w_perm, "bias_misc": bias, "gains": gains, "s64": s64,
        "pe4": jnp.stack([w["nsa_pe_k"][l].reshape(CMP_BLOCK // 4, 256), w["nsa_pe_v"][l].reshape(CMP_BLOCK // 4, 256)]),
        "w1k": w["nsa_w1k"][l].astype(BF16), "w1v": w["nsa_w1v"][l].astype(BF16),
        "pe4t": jnp.stack([w["nsa_pe_k"][l].T.reshape(HEAD_DIM // 4, 256), w["nsa_pe_v"][l].T.reshape(HEAD_DIM // 4, 256)]),
        "w1kt": by_feature(w["nsa_w1k"][l]), "w1vt": by_feature(w["nsa_w1v"][l]),
        "w2k": w2pad(w["nsa_w2k"][l]), "w2v": w2pad(w["nsa_w2v"][l]),
        "gkc": jnp.tile(w["nsa_gkc"][l], 2)[None, :],
        "g_mix": g_mix, "w_o": w_o, "g_ffn": w["g_ffn"][l][None, :],
        "w_gate": w["w_gate"][l].astype(BF16), "w_up": w["w_up"][l].astype(BF16), "w_down": w["w_down"][l].astype(BF16),
    }


def _constants(t, tk, g_sel, page):
    perm = _nsa_lane_perm()
    egate = np.zeros((3, LANES, W_NSA), np.float32)
    for lane in range(W_NSA):
        h = int(perm[lane]) // HEAD_DIM
        for j in range(3):
            egate[j, H_FOX + 3 * h + j, lane] = 1.0
    nb = t // CMP_BLOCK
    eb = np.zeros((G_NSA, LANES, t), np.float32)
    for g in range(G_NSA):
        for n in range(nb):
            eb[g, g * HEAD_DIM + n, n * CMP_BLOCK:(n + 1) * CMP_BLOCK] = 1.0
    bps = g_sel * page // CMP_BLOCK
    ee = np.zeros((bps, g_sel * page), np.float32)
    for n in range(bps):
        ee[n, n * CMP_BLOCK:(n + 1) * CMP_BLOCK] = 1.0
    eb = eb.reshape(G_NSA, LANES, t // tk, tk).transpose(0, 2, 1, 3)
    return {"egate": jnp.asarray(egate, BF16), "eb_prompt": jnp.asarray(eb, BF16), "ee_decode": jnp.asarray(ee, BF16)}


def _nsa_tk(t):
    return 2 * LANES if t % (2 * LANES) == 0 else LANES


def _pick(n, prefs):
    for p in prefs:
        if n % p == 0:
            return p
    return n


def _prompt_layer(x3, lw, tabs, consts):
    b, t, d = x3.shape
    n = b * t
    tn = _pick(t, (256, 128))
    (qf, fkv, qn, cmpr, sel, win, qs, skv, misc) = _proj(x3.reshape(n, d), lw, tabs["cos_p"], tabs["sin_p"], t // tn, tn)
    tq = _pick(t, (256, 128))
    tk = tq
    misc3 = misc.reshape(b, t, LANES)
    ccol, crow = _cumsum(misc3, tk)
    o_fox = _fox_prompt(qf.reshape(b, t, W_FOX), fkv.reshape(b, t, 2 * W_FOX), ccol, crow, tq, tk)
    o_stk = _stick_prompt(qs.reshape(b, t, W_STK), skv.reshape(b, t, 2 * W_STK), tq, tk)
    kc, vc = _compress_contig(cmpr.reshape(b, t, 2 * W_GRP), lw, tabs["cos_bp"], tabs["sin_bp"])
    o_nsa = _nsa_prompt(qn.reshape(b, t, W_NSA), kc, vc, sel.reshape(b, t, 2 * W_GRP), win.reshape(b, t, 2 * W_GRP),
                        misc3, consts, LANES, _nsa_tk(t))
    tnf = _pick(n, (512, 256, 128))
    x_out = _mix_ffn(x3.reshape(n, d), o_fox.reshape(n, W_FOX), o_nsa.reshape(n, W_NSA), o_stk.reshape(n, W_STK),
                     lw, tnf, _pick(lw["w_gate"].shape[1], (1408, 1024, 512, 256, 128)))
    n_win = min(WINDOW, t)
    rows = (fkv.reshape(b, t, 2, H_FOX, HEAD_DIM), misc3[:, :, :H_FOX], cmpr.reshape(b, t, 2, G_NSA, HEAD_DIM),
            sel.reshape(b, t, 2, G_NSA, HEAD_DIM), win.reshape(b, t, 2, G_NSA, HEAD_DIM)[:, t - n_win:],
            skv.reshape(b, t, 2, H_STK, HEAD_DIM))
    return x_out.reshape(b, t, d), rows


def _sample_layer(x3, l, lw, tabs, consts, caches, page_table, g_pages):
    db, tqn, d = x3.shape
    n = db * tqn
    n_pages = page_table.shape[1]
    page = caches["fox"].shape[4]
    past = n_pages * page
    (qf, fkv, qn, cmpr, sel, win, qs, skv, misc) = _proj(x3.reshape(n, d), lw, tabs["cos_s"], tabs["sin_s"], 1, n)
    misc3 = misc.reshape(db, tqn, LANES)

    def new_page(rows3):
        w_ = rows3.shape[2] // 2
        r = rows3.reshape(db, tqn, 2, w_).transpose(0, 2, 3, 1)
        return jnp.pad(r, ((0, 0), (0, 0), (0, 0), (0, page - tqn)))

    fkv3 = fkv.reshape(db, tqn, 2 * W_FOX)
    lfn = jnp.pad(jnp.swapaxes(misc3[:, :, :H_FOX], 1, 2), ((0, 0), (0, 8 - H_FOX), (0, LANES - tqn)))
    o_fox = _fox_decode(qf.reshape(db, tqn, W_FOX), caches["fox"], caches["logf"], l, page_table,
                        new_page(fkv3), lfn, g_pages["fox"])
    skv3 = skv.reshape(db, tqn, 2 * W_STK)
    o_stk = _stick_decode(qs.reshape(db, tqn, W_STK), caches["stk"], l, page_table, new_page(skv3), g_pages["stk"])

    cmp3 = cmpr.reshape(db, tqn, 2 * W_GRP)
    kc_p, vc_p = _compress_paged(caches["cmp"], l, page_table, lw, tabs["cos_bs"], tabs["sin_bs"], g_pages["cmp"])
    new_blk = jnp.pad(cmp3, ((0, 0), (0, CMP_BLOCK - tqn), (0, 0))).reshape(1, db * CMP_BLOCK, 2 * W_GRP)
    kc_n, vc_n = _compress_contig(new_blk, lw, tabs["cos_bn"], tabs["sin_bn"])
    nb = past // CMP_BLOCK + 1
    nbp = -(-nb // LANES) * LANES
    cat = lambda a, c: jnp.pad(jnp.concatenate([a, c.reshape(db, 1, LANES)], axis=1), ((0, 0), (0, nbp - nb), (0, 0)))
    kc = cat(kc_p, kc_n)
    vc = cat(vc_p, vc_n)
    win3 = win.reshape(db, tqn, 2 * W_GRP)
    win_all = jnp.concatenate([caches["win"][l], win3], axis=1)
    wk = win_all.shape[1]
    wkp = -(-wk // LANES) * LANES
    q3 = qn.reshape(db, tqn, W_NSA)
    o_c, o_w, selm = _nsa_sample_small(q3, kc, vc, jnp.pad(win_all, ((0, 0), (0, wkp - wk), (0, 0))),
                                       nb, past, past + tqn - wk)
    sel3 = sel.reshape(db, tqn, 2 * W_GRP)
    o_nsa = _sel_decode(q3, caches["sel"], l, page_table, new_page(sel3), selm, o_c, o_w, misc3, consts, g_pages["sel"])

    x_out = _mix_ffn(x3.reshape(n, d), o_fox.reshape(n, W_FOX), o_nsa.reshape(n, W_NSA), o_stk.reshape(n, W_STK),
                     lw, n, _pick(lw["w_gate"].shape[1], (1408, 1024, 512, 256, 128)))
    n_win = caches["win"].shape[2]
    rows = (fkv.reshape(db, tqn, 2, H_FOX, HEAD_DIM), misc3[:, :, :H_FOX], cmpr.reshape(db, tqn, 2, G_NSA, HEAD_DIM),
            sel.reshape(db, tqn, 2, G_NSA, HEAD_DIM), win_all[:, wk - n_win:].reshape(db, n_win, 2, G_NSA, HEAD_DIM),
            skv.reshape(db, tqn, 2, H_STK, HEAD_DIM))
    return x_out.reshape(db, tqn, d), rows


def kernel(x_prompt, x_sample, cache_fox_kv, cache_fox_logf, cache_nsa_cmp_kv, cache_nsa_sel_kv, state_nsa_win_kv, cache_stk_kv, page_table, g_attn, w_in, b_fox_f, b_nsa_gate, fox_gq, fox_gk, nsa_gq, nsa_gkc, nsa_gks, nsa_gkw, nsa_pe_k, nsa_w1k, nsa_w2k, nsa_pe_v, nsa_w1v, nsa_w2v, g_mix, w_o, g_ffn, w_gate, w_up, w_down):
    weights = dict(g_attn=g_attn, w_in=w_in, b_fox_f=b_fox_f, b_nsa_gate=b_nsa_gate, fox_gq=fox_gq, fox_gk=fox_gk,
                   nsa_gq=nsa_gq, nsa_gkc=nsa_gkc, nsa_gks=nsa_gks, nsa_gkw=nsa_gkw, nsa_pe_k=nsa_pe_k,
                   nsa_w1k=nsa_w1k, nsa_w2k=nsa_w2k, nsa_pe_v=nsa_pe_v, nsa_w1v=nsa_w1v, nsa_w2v=nsa_w2v,
                   g_mix=g_mix, w_o=w_o, g_ffn=g_ffn, w_gate=w_gate, w_up=w_up, w_down=w_down)
    depth = w_in.shape[0]
    b, t, d = x_prompt.shape
    db, tqn, _ = x_sample.shape
    n_pool, page = cache_fox_kv.shape[1], cache_fox_kv.shape[2]
    n_pages = page_table.shape[1]
    past = n_pages * page
    assert t % LANES == 0 and t // CMP_BLOCK <= HEAD_DIM and tqn == 8 and page == LANES
    n_win_s = state_nsa_win_kv.shape[2]

    def paged(cache, w_):
        return jnp.transpose(cache, (0, 1, 3, 4, 5, 2)).reshape(depth, n_pool, 2, w_, page)

    caches = {
        "fox": paged(cache_fox_kv, W_FOX),
        "logf": jnp.swapaxes(cache_fox_logf, 2, 3),
        "cmp": paged(cache_nsa_cmp_kv, W_GRP),
        "sel": paged(cache_nsa_sel_kv, W_GRP),
        "stk": paged(cache_stk_kv, W_STK),
        "win": state_nsa_win_kv.reshape(depth, db, n_win_s, 2 * W_GRP),
    }
    g_pages = {k: _pick(n_pages, (32, 16, 8, 4, 2, 1)) for k in ("fox", "stk", "sel")}
    g_pages["cmp"] = _pick(n_pages, (32, 16, 8, 4, 2, 1))
    consts = _constants(t, _nsa_tk(t), g_pages["sel"], page)

    nbp_blocks = t // CMP_BLOCK
    nbs_blocks = past // CMP_BLOCK
    bend = lambda n0, cnt: (n0 + jnp.arange(cnt, dtype=jnp.int32)) * CMP_BLOCK + (CMP_BLOCK - 1)
    pos_all = jnp.concatenate([
        jnp.arange(t, dtype=jnp.int32),
        jnp.tile(past + jnp.arange(tqn, dtype=jnp.int32), db),
        bend(0, nbp_blocks), bend(0, nbs_blocks).reshape(-1, g_pages["cmp"], 2).transpose(0, 2, 1).reshape(-1),
        jnp.tile(bend(nbs_blocks, 1), db),
    ])
    pad = (-pos_all.shape[0]) % 8
    cos_all, sin_all = _rope_tables(jnp.pad(pos_all, (0, pad)))
    offs = np.cumsum([0, t, db * tqn, nbp_blocks, nbs_blocks, db])
    names = ("p", "s", "bp", "bs", "bn")
    tabs = {}
    for i, nm in enumerate(names):
        tabs["cos_" + nm] = cos_all[offs[i]:offs[i + 1]]
        tabs["sin_" + nm] = sin_all[offs[i]:offs[i + 1]]

    xp, xs = x_prompt, x_sample
    rows_p, rows_s = [], []
    for l in range(depth):
        lw = _layer_weights(l, weights)
        xp, rp = _prompt_layer(xp, lw, tabs, consts)
        xs, rs = _sample_layer(xs, l, lw, tabs, consts, caches, page_table, g_pages)
        rows_p.append(rp)
        rows_s.append(rs)
    outs = [xp, xs]
    for i in range(6):
        outs.append(jnp.stack([r[i] for r in rows_p], axis=0))
        outs.append(jnp.stack([r[i] for r in rows_s], axis=0))
    return tuple(outs)
```

```python
import functools
import math

import numpy as np
import jax
import jax.numpy as jnp
from jax import lax
from jax.experimental import pallas as pl
from jax.experimental.pallas import tpu as pltpu

F32 = jnp.float32
BF16 = jnp.bfloat16

HEAD_DIM = 64
H_FOX = 4
H_NSA = 8
G_NSA = 2
R_NSA = H_NSA // G_NSA
H_STK = 4
CMP_BLOCK = 64
CMP_SHIFT = 6
CMP_HIDDEN = 256
TOP_N = 16
WINDOW = 512
FORCE_BONUS = 1.0e4
ROPE_THETA = 10000.0
EPS = 1e-6
LANES = 128
SCALE = HEAD_DIM ** -0.5

W_FOX = H_FOX * HEAD_DIM
W_NSA = H_NSA * HEAD_DIM
W_GRP = G_NSA * HEAD_DIM
W_STK = H_STK * HEAD_DIM
N_MISC = H_FOX + 3 * H_NSA

C_FQ, C_FKV, C_NQ, C_CMP, C_SEL, C_WIN, C_SQ, C_SKV, C_MISC = 0, 256, 768, 1280, 1536, 1792, 2048, 2304, 2816
W_PROJ = 2944

NSA_HEAD_ORDER = (0, 4, 1, 5, 2, 6, 3, 7)

VMEM_LIMIT = 56 * 1024 * 1024


def _cparams(*sem):
    return pltpu.CompilerParams(dimension_semantics=sem, vmem_limit_bytes=VMEM_LIMIT)


def _iota(shape, dim):
    return lax.broadcasted_iota(jnp.int32, shape, dim)


def _log_sigmoid(z):
    return jnp.minimum(z, 0.0) - jnp.log(1.0 + jnp.exp(-jnp.abs(z)))


def _dot(a, b):
    return jnp.dot(a, b, preferred_element_type=F32)


def _dot_nt(a, b):
    return lax.dot_general(a, b, (((1,), (1,)), ((), ())), preferred_element_type=F32)


def _split3(x):
    x1 = x.astype(BF16)
    r1 = x - x1.astype(F32)
    x2 = r1.astype(BF16)
    x3 = (r1 - x2.astype(F32)).astype(BF16)
    return x1, x2, x3


def _dot_exact_rhs(x, e):
    x1, x2, x3 = _split3(x)
    return _dot(x1, e) + _dot(x2, e) + _dot(x3, e)


def _dot_exact_lhs(e, x):
    x1, x2, x3 = _split3(x)
    return _dot(e, x1) + _dot(e, x2) + _dot(e, x3)


def _head_norm(y, s64, gain):
    w = y.shape[1]
    y2 = (y * y).astype(BF16)
    outs = []
    for c in range(0, w, 256):
        cw = min(256, w - c)
        ms = _dot(y2[:, c:c + cw], s64[:cw, :cw])
        outs.append(y[:, c:c + cw] * lax.rsqrt(ms + EPS))
    yn = outs[0] if len(outs) == 1 else jnp.concatenate(outs, axis=1)
    return yn * gain


def _rope(y, cos, sin):
    w = y.shape[1]
    lane = _iota(y.shape, 1)
    first = (lane & 63) < 32
    sw = jnp.where(first, pltpu.roll(y, w - 32, 1), pltpu.roll(y, 32, 1))
    if w > LANES:
        cos = jnp.concatenate([cos] * (w // LANES), axis=1)
        sin = jnp.concatenate([sin] * (w // LANES), axis=1)
    return y * cos + sw * sin


def _suffix_excl_lanes(x):
    lane = _iota(x.shape, 1)
    y = x
    s = 1
    while s < LANES:
        y = y + jnp.where(lane + s < LANES, pltpu.roll(y, LANES - s, 1), 0.0)
        s *= 2
    return y - x, y[:, 0:1]


def _stack_heads(q, n_heads):
    lane = _iota(q.shape, 1)
    zero = jnp.zeros_like(q)
    return jnp.concatenate([jnp.where((lane >> 6) == h, q, zero) for h in range(n_heads)], axis=0)


def _unstack_heads(o, n_heads, t):
    lane = _iota((t, o.shape[1]), 1)
    out = jnp.zeros((t, o.shape[1]), F32)
    for h in range(n_heads):
        out = out + jnp.where((lane >> 6) == h, o[h * t:(h + 1) * t], 0.0)
    return out


def _stack_nsa(q):
    t = q.shape[0]
    lane = _iota((t, LANES), 1)
    blocks = []
    for c in range(R_NSA):
        qc = q[:, c * LANES:(c + 1) * LANES]
        for g in range(G_NSA):
            blocks.append(jnp.where((lane >> 6) == g, qc, jnp.zeros_like(qc)))
    return jnp.concatenate(blocks, axis=0)


def _unstack_nsa(o, t):
    lane = _iota((t, LANES), 1)
    chunks = []
    for c in range(R_NSA):
        a = o[(2 * c) * t:(2 * c + 1) * t]
        b = o[(2 * c + 1) * t:(2 * c + 2) * t]
        chunks.append(jnp.where((lane >> 6) == 0, a, b))
    return jnp.concatenate(chunks, axis=1)


def _softmax_update(s, mask, m_ref, l_ref, acc_ref, vt):
    if mask is not None:
        s = jnp.where(mask, s, -jnp.inf)
    m_old = m_ref[...]
    m_new = jnp.maximum(m_old, jnp.max(s, axis=1, keepdims=True))
    m_safe = jnp.where(m_new > -jnp.inf, m_new, 0.0)
    alpha = jnp.exp(m_old - m_safe)
    p = jnp.exp(s - m_safe)
    l_ref[...] = alpha * l_ref[...] + jnp.sum(p, axis=1, keepdims=True)
    acc_ref[...] = alpha * acc_ref[...] + _dot_nt(p.astype(BF16), vt)
    m_ref[...] = m_new


def _tile_lanes(x, w):
    return x if w == LANES else jnp.concatenate([x] * (w // LANES), axis=1)


def _strip_softmax(s, mask, r0, strip, m_ref, l_ref, a_ref, p_ref, guard):
    rows = pl.ds(r0, strip)
    if mask is not None:
        s = jnp.where(mask, s, -jnp.inf)
    m_old = m_ref[rows, :]
    m_new = jnp.maximum(m_old, jnp.max(s, axis=1, keepdims=True))
    m_use = jnp.where(m_new > -jnp.inf, m_new, 0.0) if guard else m_new
    alpha = jnp.exp(m_old - m_use)
    p = jnp.exp(s - _tile_lanes(m_use, s.shape[1]))
    l_ref[rows, :] = alpha * l_ref[rows, :] + jnp.sum(p, axis=1, keepdims=True)
    m_ref[rows, :] = m_new
    a_ref[rows, :] = alpha
    p_ref[rows, :] = p.astype(BF16)


def _gate_expand(misc, eg_ref):
    m1, m2, m3 = _split3(misc)
    outs = []
    for j in range(3):
        e = eg_ref[j]
        outs.append(_dot(m1, e) + _dot(m2, e) + _dot(m3, e))
    return outs


def _rope_table_kernel(pos_ref, inv_ref, sgn_ref, cos_ref, sin_ref):
    ang = pos_ref[...] * inv_ref[...]
    cos_ref[...] = jnp.cos(ang)
    sin_ref[...] = jnp.sin(ang) * sgn_ref[...]


def _rope_tables(pos):
    p = pos.shape[0]
    half = HEAD_DIM // 2
    inv = ROPE_THETA ** (-jnp.arange(half, dtype=F32) / half)
    inv128 = jnp.tile(inv, 4)[None, :]
    sgn128 = jnp.tile(jnp.concatenate([-jnp.ones((half,), F32), jnp.ones((half,), F32)]), 2)[None, :]
    return pl.pallas_call(
        _rope_table_kernel,
        out_shape=(jax.ShapeDtypeStruct((p, LANES), F32), jax.ShapeDtypeStruct((p, LANES), F32)),
        name="rope_tables",
    )(pos.astype(F32)[:, None], inv128, sgn128)


def _proj_kernel(x_ref, g_ref, w_ref, bias_ref, gains_ref, cos_ref, sin_ref, s64_ref,
                 qf_ref, fkv_ref, qn_ref, cmp_ref, sel_ref, win_ref, qs_ref, skv_ref, misc_ref):
    x = x_ref[...]
    ms = jnp.mean(x * x, axis=-1, keepdims=True)
    xn = (x * lax.rsqrt(ms + EPS) * g_ref[...]).astype(BF16)
    cos = cos_ref[...]
    sin = sin_ref[...]
    s64 = s64_ref[...]

    def mm(c0, w):
        return _dot_nt(xn, w_ref[c0:c0 + w, :])

    qf_ref[...] = (_head_norm(mm(C_FQ, W_FOX), s64, gains_ref[0:1, :W_FOX]) * SCALE).astype(BF16)
    kv = mm(C_FKV, 2 * W_FOX)
    fkv_ref[:, :W_FOX] = _head_norm(kv[:, :W_FOX], s64, gains_ref[1:2, :W_FOX])
    fkv_ref[:, W_FOX:] = kv[:, W_FOX:]
    qn = _rope(_head_norm(mm(C_NQ, W_NSA), s64, gains_ref[2:3, :W_NSA]), cos, sin)
    qn_ref[...] = (qn * SCALE).astype(BF16)
    cmp_ref[...] = mm(C_CMP, 2 * W_GRP)
    sel = mm(C_SEL, 2 * W_GRP)
    sel_ref[:, :W_GRP] = _rope(_head_norm(sel[:, :W_GRP], s64, gains_ref[3:4, :W_GRP]), cos, sin)
    sel_ref[:, W_GRP:] = sel[:, W_GRP:]
    win = mm(C_WIN, 2 * W_GRP)
    win_ref[:, :W_GRP] = _rope(_head_norm(win[:, :W_GRP], s64, gains_ref[4:5, :W_GRP]), cos, sin)
    win_ref[:, W_GRP:] = win[:, W_GRP:]
    qs_ref[...] = (mm(C_SQ, W_STK) * SCALE).astype(BF16)
    skv_ref[...] = mm(C_SKV, 2 * W_STK)
    c = mm(C_MISC, LANES) + bias_ref[...]
    lane = _iota(c.shape, 1)
    misc_ref[...] = jnp.where(lane < H_FOX, _log_sigmoid(c), 1.0 / (1.0 + jnp.exp(-c)))


def _proj(x2d, lw, cos, sin, n_pos_blocks, tn):
    n, d = x2d.shape
    nt = n // tn
    row = lambda i: (i, 0)
    const = lambda i: (0, 0)
    widths = (W_FOX, 2 * W_FOX, W_NSA, 2 * W_GRP, 2 * W_GRP, 2 * W_GRP, W_STK, 2 * W_STK, LANES)
    dtypes = (BF16, F32, BF16, F32, F32, F32, BF16, F32, F32)
    out_shape = tuple(jax.ShapeDtypeStruct((n, w), dt) for w, dt in zip(widths, dtypes))
    out_specs = tuple(pl.BlockSpec((tn, w), row) for w in widths)
    pos_map = lambda i: (i % n_pos_blocks, 0)
    return pl.pallas_call(
        _proj_kernel,
        grid=(nt,),
        in_specs=[
            pl.BlockSpec((tn, d), row),
            pl.BlockSpec((1, d), const),
            pl.BlockSpec((W_PROJ, d), const),
            pl.BlockSpec((1, LANES), const),
            pl.BlockSpec((8, W_NSA), const),
            pl.BlockSpec((tn, LANES), pos_map),
            pl.BlockSpec((tn, LANES), pos_map),
            pl.BlockSpec((256, 256), const),
        ],
        out_specs=out_specs,
        out_shape=out_shape,
        compiler_params=_cparams("arbitrary"),
        name="proj",
    )(x2d, lw["g_attn"], lw["w_in"], lw["bias_misc"], lw["gains"], cos, sin, lw["s64"])


def _cumsum_kernel(misc_ref, ccol_ref, crow_ref, *, blk):
    t = misc_ref.shape[1]
    r = _iota((blk, blk), 0)
    c = _iota((blk, blk), 1)
    tri = jnp.where(r >= c, 1.0, 0.0).astype(BF16)
    carry = jnp.zeros((1, LANES), F32)
    for i in range(t // blk):
        x = misc_ref[0, i * blk:(i + 1) * blk, :]
        cs = _dot_exact_lhs(tri, x) + carry
        carry = cs[blk - 1:blk, :]
        ccol_ref[0, i * blk:(i + 1) * blk, :] = cs
        crow_ref[0, i] = cs.T[0:8, :]


def _cumsum(misc3, blk):
    b, t, _ = misc3.shape
    return pl.pallas_call(
        functools.partial(_cumsum_kernel, blk=blk),
        grid=(b,),
        in_specs=[pl.BlockSpec((1, t, LANES), lambda i: (i, 0, 0))],
        out_specs=(pl.BlockSpec((1, t, LANES), lambda i: (i, 0, 0)),
                   pl.BlockSpec((1, t // blk, 8, blk), lambda i: (i, 0, 0, 0))),
        out_shape=(jax.ShapeDtypeStruct((b, t, LANES), F32),
                   jax.ShapeDtypeStruct((b, t // blk, 8, blk), F32)),
        compiler_params=_cparams("arbitrary"),
        name="logf_cumsum",
    )(misc3)


def _fox_prompt_kernel(q_ref, kv_ref, ccol_ref, crow_ref, o_ref, kb, vb, q4_s, cq_s, s_s, p_s, m_s, l_s, a_s, acc_s,
                       *, tq, strip):
    qi = pl.program_id(1)

    @pl.when(qi == 0)
    def _():
        kb[...] = kv_ref[0, :, :W_FOX].astype(BF16)
        vb[...] = kv_ref[0, :, W_FOX:].astype(BF16)

    q4_s[...] = _stack_heads(q_ref[0], H_FOX)
    cc = ccol_ref[0]
    cq_s[...] = jnp.concatenate([jnp.broadcast_to(cc[:, h:h + 1], (tq, LANES)) for h in range(H_FOX)], axis=0)
    m_s[...] = jnp.full(m_s.shape, -jnp.inf, F32)
    l_s[...] = jnp.zeros(l_s.shape, F32)
    acc_s[...] = jnp.zeros(acc_s.shape, F32)
    q0 = qi * tq

    def tile(kj, diag):
        k0 = pl.multiple_of(kj * tq, tq)
        s_s[...] = _dot_nt(q4_s[...], kb[pl.ds(k0, tq), :])
        cr = crow_ref[0, kj]
        kpos = k0 + _iota((1, tq), 1)
        for h in range(H_FOX):
            ck = cr[h:h + 1, :]

            def strip_body(i, c, h=h, ck=ck):
                r0 = h * tq + i * strip
                s = s_s[pl.ds(r0, strip), :] + _tile_lanes(cq_s[pl.ds(r0, strip), :], tq) - ck
                mask = (kpos <= q0 + i * strip + _iota((strip, 1), 0)) if diag else None
                _strip_softmax(s, mask, r0, strip, m_s, l_s, a_s, p_s, False)
                return c

            for i in range(tq // strip):
                strip_body(i, 0)
        acc_s[...] = _tile_lanes(a_s[...], W_FOX) * acc_s[...] + _dot(p_s[...], vb[pl.ds(k0, tq), :])

    def body(kj, c):
        tile(kj, False)
        return c

    lax.fori_loop(0, qi, body, 0)
    tile(qi, True)
    o4 = acc_s[...] / _tile_lanes(jnp.maximum(l_s[...], 1e-30), W_FOX)
    o_ref[0] = _unstack_heads(o4, H_FOX, tq)


def _fox_prompt(q3, kv3, ccol, crow, tq, tk):
    b, t, _ = q3.shape
    assert tq == tk
    rows = H_FOX * tq
    return pl.pallas_call(
        functools.partial(_fox_prompt_kernel, tq=tq, strip=32),
        grid=(b, t // tq),
        in_specs=[
            pl.BlockSpec((1, tq, W_FOX), lambda i, j: (i, j, 0)),
            pl.BlockSpec((1, t, 2 * W_FOX), lambda i, j: (i, 0, 0)),
            pl.BlockSpec((1, tq, LANES), lambda i, j: (i, j, 0)),
            pl.BlockSpec((1, t // tk, 8, tk), lambda i, j: (i, 0, 0, 0)),
        ],
        out_specs=pl.BlockSpec((1, tq, W_FOX), lambda i, j: (i, j, 0)),
        out_shape=jax.ShapeDtypeStruct((b, t, W_FOX), F32),
        scratch_shapes=[
            pltpu.VMEM((t, W_FOX), BF16), pltpu.VMEM((t, W_FOX), BF16),
            pltpu.VMEM((rows, W_FOX), BF16), pltpu.VMEM((rows, LANES), F32),
            pltpu.VMEM((rows, tq), F32), pltpu.VMEM((rows, tq), BF16),
            pltpu.VMEM((rows, LANES), F32), pltpu.VMEM((rows, LANES), F32), pltpu.VMEM((rows, LANES), F32),
            pltpu.VMEM((rows, W_FOX), F32),
        ],
        compiler_params=_cparams("arbitrary", "arbitrary"),
        name="fox_prompt",
    )(q3, kv3, ccol, crow)


def _stick_prompt_kernel(q_ref, kv_ref, o_ref, kb, vb, tri_s, q4_s, z_s, hi_s, lo_s, rc_s, p_s, rs_s, r_s, acc_s,
                         *, tq, strip):
    qi = pl.program_id(1)

    @pl.when(qi == 0)
    def _():
        kb[...] = kv_ref[0, :, :W_STK].astype(BF16)
        vb[...] = kv_ref[0, :, W_STK:].astype(BF16)
        tri_s[...] = jnp.where(_iota((tq, tq), 0) > _iota((tq, tq), 1), 1.0, 0.0).astype(BF16)

    q4_s[...] = _stack_heads(q_ref[0], H_STK)
    r_s[...] = jnp.zeros(r_s.shape, F32)
    acc_s[...] = jnp.zeros(acc_s.shape, F32)
    q0 = qi * tq
    n_strips = H_STK * tq // strip

    def tile(kj, diag):
        k0 = pl.multiple_of(kj * tq, tq)
        z_s[...] = _dot_nt(q4_s[...], kb[pl.ds(k0, tq), :])
        kpos = k0 + _iota((1, tq), 1)

        def strip_mask(i):
            tq_off = (i * strip) & (tq - 1)
            return kpos < q0 + tq_off + _iota((strip, 1), 0)

        def pass1(i, c):
            rows = pl.ds(i * strip, strip)
            z = z_s[rows, :]
            ls = _log_sigmoid(z)
            lk = ls - z
            if diag:
                lk = jnp.where(strip_mask(i), lk, 0.0)
            hi = lk.astype(BF16)
            hi_s[rows, :] = hi
            lo_s[rows, :] = (lk - hi.astype(F32)).astype(BF16)
            z_s[rows, :] = ls
            rs_s[rows, :] = jnp.sum(lk, axis=1, keepdims=True)
            return c

        for i in range(n_strips):
            pass1(i, 0)
        tri = tri_s[...]
        rc_s[...] = _dot(hi_s[...], tri) + _dot(lo_s[...], tri)

        def pass2(i, c):
            rows = pl.ds(i * strip, strip)
            a = jnp.exp(z_s[rows, :] + rc_s[rows, :] + r_s[rows, :])
            if diag:
                a = jnp.where(strip_mask(i), a, 0.0)
            p_s[rows, :] = a.astype(BF16)
            return c

        for i in range(n_strips):
            pass2(i, 0)
        acc_s[...] = acc_s[...] + _dot(p_s[...], vb[pl.ds(k0, tq), :])
        r_s[...] = r_s[...] + rs_s[...]

    tile(qi, True)

    def body(it, c):
        tile(qi - 1 - it, False)
        return c

    lax.fori_loop(0, qi, body, 0)
    o_ref[0] = _unstack_heads(acc_s[...], H_STK, tq)


def _stick_prompt(q3, kv3, tq, tk):
    b, t, _ = q3.shape
    assert tq == tk and (tq & (tq - 1)) == 0
    rows = H_STK * tq
    return pl.pallas_call(
        functools.partial(_stick_prompt_kernel, tq=tq, strip=32),
        grid=(b, t // tq),
        in_specs=[
            pl.BlockSpec((1, tq, W_STK), lambda i, j: (i, j, 0)),
            pl.BlockSpec((1, t, 2 * W_STK), lambda i, j: (i, 0, 0)),
        ],
        out_specs=pl.BlockSpec((1, tq, W_STK), lambda i, j: (i, j, 0)),
        out_shape=jax.ShapeDtypeStruct((b, t, W_STK), F32),
        scratch_shapes=[
            pltpu.VMEM((t, W_STK), BF16), pltpu.VMEM((t, W_STK), BF16), pltpu.VMEM((tq, tq), BF16),
            pltpu.VMEM((rows, W_STK), BF16),
            pltpu.VMEM((rows, tq), F32), pltpu.VMEM((rows, tq), BF16), pltpu.VMEM((rows, tq), BF16),
            pltpu.VMEM((rows, tq), F32), pltpu.VMEM((rows, tq), BF16),
            pltpu.VMEM((rows, 1), F32), pltpu.VMEM((rows, 1), F32),
            pltpu.VMEM((rows, W_STK), F32),
        ],
        compiler_params=_cparams("arbitrary", "arbitrary"),
        name="stick_prompt",
    )(q3, kv3)


def _compress_body(x_refs, pe_ref, w1k_ref, w1v_ref, w2k_ref, w2v_ref, gkc_ref, cos_ref, sin_ref, s64_ref,
                   kc_ref, vc_ref, xs):
    off = 0
    for r in x_refs:
        rows = r.shape[-2]
        v = r[...].reshape(rows, 2 * W_GRP)
        xs[0, off:off + rows, :] = v[:, :LANES]
        xs[1, off:off + rows, :] = v[:, LANES:]
        off += rows
    nblk = off // CMP_BLOCK
    lane = _iota((nblk, LANES), 1)
    lo_half = lane < HEAD_DIM
    hk = jnp.zeros((2 * nblk, CMP_HIDDEN), F32)
    hv = jnp.zeros((2 * nblk, CMP_HIDDEN), F32)
    for i4 in range(CMP_BLOCK // 4):
        x = [[xs[h, pl.ds(4 * i4 + j, nblk, stride=CMP_BLOCK), :] for h in range(2)] for j in range(4)]
        sw = [[pltpu.roll(v, HEAD_DIM, 1) for v in xj] for xj in x]

        def chunk_at(j, c, pos):
            return x[j][c // 2] if (c % 2) == pos else sw[j][c // 2]

        packed = []
        for c in range(4):
            left = jnp.where(lo_half, chunk_at(0, c, 0), chunk_at(1, c, 1))
            right = jnp.where(lo_half, chunk_at(2, c, 0), chunk_at(3, c, 1))
            packed.append(jnp.concatenate([left, right], axis=1))
        ak = jnp.concatenate([packed[0], packed[1]], axis=0) + pe_ref[0, i4:i4 + 1, :]
        av = jnp.concatenate([packed[2], packed[3]], axis=0) + pe_ref[1, i4:i4 + 1, :]
        hk = hk + _dot(ak.astype(BF16), w1k_ref[256 * i4:256 * (i4 + 1), :])
        hv = hv + _dot(av.astype(BF16), w1v_ref[256 * i4:256 * (i4 + 1), :])
    gk = jax.nn.gelu(hk).astype(BF16)
    gv = jax.nn.gelu(hv).astype(BF16)
    kc = _dot(gk[:nblk], w2k_ref[0]) + _dot(gk[nblk:], w2k_ref[1])
    kc = _rope(_head_norm(kc, s64_ref[...], gkc_ref[...]), cos_ref[...], sin_ref[...])
    kc_ref[...] = kc.reshape(kc_ref.shape)
    vc = _dot(gv[:nblk], w2v_ref[0]) + _dot(gv[nblk:], w2v_ref[1])
    vc_ref[...] = vc.reshape(vc_ref.shape)


def _compress_kernel(*refs, n_x, n_prefetch):
    refs = refs[n_prefetch:]
    _compress_body(refs[:n_x], *refs[n_x:])


def _compress_weight_specs(nmap):
    return [
        pl.BlockSpec((2, CMP_BLOCK // 4, 256), nmap(3)),
        pl.BlockSpec((CMP_BLOCK * HEAD_DIM, CMP_HIDDEN), nmap(2)),
        pl.BlockSpec((CMP_BLOCK * HEAD_DIM, CMP_HIDDEN), nmap(2)),
        pl.BlockSpec((2, CMP_HIDDEN, LANES), nmap(3)),
        pl.BlockSpec((2, CMP_HIDDEN, LANES), nmap(3)),
        pl.BlockSpec((1, LANES), nmap(2)),
    ]


def _compress_contig(x3, lw, cos, sin):
    b, t, _ = x3.shape
    nblk = t // CMP_BLOCK
    nmap = lambda n: (lambda i: (0,) * n)
    return pl.pallas_call(
        functools.partial(_compress_kernel, n_x=1, n_prefetch=0),
        grid=(b,),
        in_specs=[pl.BlockSpec((1, t, 2 * W_GRP), lambda i: (i, 0, 0))] + _compress_weight_specs(nmap) + [
            pl.BlockSpec((nblk, LANES), nmap(2)),
            pl.BlockSpec((nblk, LANES), nmap(2)),
            pl.BlockSpec((256, 256), nmap(2)),
        ],
        out_specs=(pl.BlockSpec((1, nblk, LANES), lambda i: (i, 0, 0)),
                   pl.BlockSpec((1, nblk, LANES), lambda i: (i, 0, 0))),
        out_shape=(jax.ShapeDtypeStruct((b, nblk, LANES), F32), jax.ShapeDtypeStruct((b, nblk, LANES), F32)),
        scratch_shapes=[pltpu.VMEM((2, t, LANES), F32)],
        compiler_params=_cparams("arbitrary"),
        name="nsa_compress",
    )(x3, lw["pe4"], lw["w1k"], lw["w1v"], lw["w2k"], lw["w2v"], lw["gkc"], cos, sin, lw["s64"])


def _block_select(score, visible, lane_n, nb, n_sel, group_lanes):
    cnt = jnp.zeros(score.shape, F32)
    n_slots = score.shape[1] // group_lanes
    lane = _iota(score.shape, 1)
    for j in range(nb):
        if n_slots == 1:
            vj = score[:, j:j + 1]
        else:
            vj = score[:, j:j + 1]
            for sl in range(1, n_slots):
                vj = jnp.where(lane >= sl * group_lanes, score[:, sl * group_lanes + j:sl * group_lanes + j + 1], vj)
        ahead = (vj > score) | ((vj == score) & (lane_n > j))
        cnt = cnt + jnp.where(ahead, 1.0, 0.0)
    return visible & (cnt < n_sel)


def _masked_softmax_rows(s, mask):
    s = jnp.where(mask, s, -jnp.inf)
    m = jnp.max(s, axis=1, keepdims=True)
    m = jnp.where(m > -jnp.inf, m, 0.0)
    e = jnp.where(mask, jnp.exp(s - m), 0.0)
    return e / jnp.maximum(jnp.sum(e, axis=1, keepdims=True), 1e-30)


def _nsa_prompt_kernel(q_ref, kc_ref, vc_ref, sel_ref, win_ref, misc_ref, eg_ref, eb_ref, o_ref,
                       selb, winb, kcs, vcs, qs_s, selm_s, me_s, s_s, p_s, m_s, l_s, a_s, acc_s, sw_s, pw_s,
                       *, tq, tk, nb, span, strip):
    qi = pl.program_id(1)

    @pl.when(qi == 0)
    def _():
        selb[...] = sel_ref[0].astype(BF16)
        winb[...] = win_ref[0].astype(BF16)
        lane = _iota((nb, LANES), 1)
        kc = kc_ref[0]
        vc = vc_ref[0]
        kcs[...] = jnp.zeros(kcs.shape, BF16)
        vcs[...] = jnp.zeros(vcs.shape, BF16)
        for g in range(G_NSA):
            keep = (lane >> 6) == g
            kcs[g * HEAD_DIM:g * HEAD_DIM + nb, :] = jnp.where(keep, kc, 0.0).astype(BF16)
            vcs[g * HEAD_DIM:g * HEAD_DIM + nb, :] = jnp.where(keep, vc, 0.0).astype(BF16)

    q = q_ref[0]
    q0 = qi * tq
    tpos = q0 + _iota((tq, 1), 0)
    lane = _iota((tq, LANES), 1)
    n_idx = lane & 63
    half = lane >> 6

    cmask = ((n_idx << CMP_SHIFT) + (CMP_BLOCK - 1) <= tpos) & (n_idx < nb)
    imp = jnp.zeros((tq, LANES), F32)
    oc_chunks = []
    for c in range(R_NSA):
        s = _dot_nt(q[:, c * LANES:(c + 1) * LANES], kcs[...])
        s = jnp.where(cmask, s, -jnp.inf)
        m0 = jnp.max(jnp.where(half == 0, s, -jnp.inf), axis=1, keepdims=True)
        m1 = jnp.max(jnp.where(half == 1, s, -jnp.inf), axis=1, keepdims=True)
        m = jnp.where(half == 0, m0, m1)
        m = jnp.where(m > -jnp.inf, m, 0.0)
        e = jnp.where(cmask, jnp.exp(s - m), 0.0)
        d0 = jnp.sum(jnp.where(half == 0, e, 0.0), axis=1, keepdims=True)
        d1 = jnp.sum(jnp.where(half == 1, e, 0.0), axis=1, keepdims=True)
        pc = e / jnp.maximum(jnp.where(half == 0, d0, d1), 1e-30)
        imp = imp + pc
        oc_chunks.append(_dot(pc.astype(BF16), vcs[...]))
    o_c = jnp.concatenate(oc_chunks, axis=1)

    visible = ((n_idx << CMP_SHIFT) <= tpos) & (n_idx < nb)
    cur = tpos >> CMP_SHIFT
    forced = (n_idx == 0) | (n_idx == cur) | (n_idx == cur - 1)
    score = jnp.where(visible, imp + FORCE_BONUS * jnp.where(forced, 1.0, 0.0), -jnp.inf)
    selected = _block_select(score, visible, n_idx, nb, min(TOP_N, nb), HEAD_DIM)
    selm_s[...] = jnp.where(selected, 1.0, 0.0).astype(BF16)

    qs_s[...] = _stack_nsa(q)
    n_blocks = G_NSA * R_NSA

    m_s[...] = jnp.full(m_s.shape, -jnp.inf, F32)
    l_s[...] = jnp.zeros(l_s.shape, F32)
    acc_s[...] = jnp.zeros(acc_s.shape, F32)

    def sel_tile(kj, diag):
        k0 = pl.multiple_of(kj * tk, tk)
        s_s[...] = _dot_nt(qs_s[...], selb[pl.ds(k0, tk), :W_GRP])
        for g in range(G_NSA):
            me_s[g] = _dot(selm_s[...], eb_ref[g, kj])
        kpos = k0 + _iota((1, tk), 1)
        for blk in range(n_blocks):

            def strip_body(i, c, blk=blk):
                off = i * strip
                r0 = blk * tq + i * strip
                mk = me_s[blk % G_NSA, pl.ds(off, strip), :] > 0.5
                if diag:
                    mk = mk & (kpos <= q0 + i * strip + _iota((strip, 1), 0))
                _strip_softmax(s_s[pl.ds(r0, strip), :], mk, r0, strip, m_s, l_s, a_s, p_s, True)
                return c

            for i in range(tq // strip):
                strip_body(i, 0)
        acc_s[...] = a_s[...] * acc_s[...] + _dot(p_s[...], selb[pl.ds(k0, tk), W_GRP:])

    def body(kj, c):
        sel_tile(kj, False)
        return c

    nk = (q0 + tq - 1) // tk + 1
    lax.fori_loop(0, nk - 1, body, 0)
    sel_tile(nk - 1, True)
    o_s = _unstack_nsa(acc_s[...] / jnp.maximum(l_s[...], 1e-30), tq)

    start = pl.multiple_of(jnp.maximum(q0 + tq - span, 0), LANES)
    sw_s[...] = _dot_nt(qs_s[...], winb[pl.ds(start, span), :W_GRP])
    kposw = start + _iota((1, span), 1)

    def wstrip(i, c):
        rows = pl.ds(i * strip, strip)
        tp = q0 + ((i * strip) & (tq - 1)) + _iota((strip, 1), 0)
        pw = _masked_softmax_rows(sw_s[rows, :], (kposw <= tp) & (kposw > tp - WINDOW))
        pw_s[rows, :] = pw.astype(BF16)
        return c

    for i in range(n_blocks * tq // strip):
        wstrip(i, 0)
    o_w = _unstack_nsa(_dot(pw_s[...], winb[pl.ds(start, span), W_GRP:]), tq)

    g_c, g_s, g_w = _gate_expand(misc_ref[0], eg_ref)
    o_ref[0] = g_c * o_c + g_s * o_s + g_w * o_w


def _nsa_prompt(q3, kc, vc, sel3, win3, misc3, consts, tq, tk):
    b, t, _ = q3.shape
    nb = t // CMP_BLOCK
    span = min(WINDOW + tq, t)
    rows = H_NSA * tq
    assert (tq & (tq - 1)) == 0
    return pl.pallas_call(
        functools.partial(_nsa_prompt_kernel, tq=tq, tk=tk, nb=nb, span=span, strip=32),
        grid=(b, t // tq),
        in_specs=[
            pl.BlockSpec((1, tq, W_NSA), lambda i, j: (i, j, 0)),
            pl.BlockSpec((1, nb, LANES), lambda i, j: (i, 0, 0)),
            pl.BlockSpec((1, nb, LANES), lambda i, j: (i, 0, 0)),
            pl.BlockSpec((1, t, 2 * W_GRP), lambda i, j: (i, 0, 0)),
            pl.BlockSpec((1, t, 2 * W_GRP), lambda i, j: (i, 0, 0)),
            pl.BlockSpec((1, tq, LANES), lambda i, j: (i, j, 0)),
            pl.BlockSpec((3, LANES, W_NSA), lambda i, j: (0, 0, 0)),
            pl.BlockSpec((G_NSA, t // tk, LANES, tk), lambda i, j: (0, 0, 0, 0)),
        ],
        out_specs=pl.BlockSpec((1, tq, W_NSA), lambda i, j: (i, j, 0)),
        out_shape=jax.ShapeDtypeStruct((b, t, W_NSA), F32),
        scratch_shapes=[
            pltpu.VMEM((t, 2 * W_GRP), BF16), pltpu.VMEM((t, 2 * W_GRP), BF16),
            pltpu.VMEM((LANES, LANES), BF16), pltpu.VMEM((LANES, LANES), BF16),
            pltpu.VMEM((rows, LANES), BF16), pltpu.VMEM((tq, LANES), BF16),
            pltpu.VMEM((G_NSA, tq, tk), F32),
            pltpu.VMEM((rows, tk), F32), pltpu.VMEM((rows, tk), BF16),
            pltpu.VMEM((rows, LANES), F32), pltpu.VMEM((rows, LANES), F32), pltpu.VMEM((rows, LANES), F32),
            pltpu.VMEM((rows, LANES), F32),
            pltpu.VMEM((rows, span), F32), pltpu.VMEM((rows, span), BF16),
        ],
        compiler_params=_cparams("arbitrary", "arbitrary"),
        name="nsa_prompt",
    )(q3, kc, vc, sel3, win3, misc3, consts["egate"], consts["eb_prompt"])


def _compress_paged_kernel(pt_ref, *refs, g_pages):
    x_refs = refs[:g_pages]
    (pe_ref, w1k_ref, w1v_ref, w2k_ref, w2v_ref, gkc_ref, cos_ref, sin_ref, s64_ref, kc_ref, vc_ref) = refs[g_pages:]
    n_sl = G_NSA * g_pages
    lo = _iota((n_sl, LANES), 1) < CMP_BLOCK
    hs = []
    for kv, w1 in enumerate((w1k_ref, w1v_ref)):
        h = jnp.zeros((2 * n_sl, CMP_HIDDEN), F32)
        for o in range(HEAD_DIM // 8):
            rows = [r[kv, g * HEAD_DIM + 8 * o:g * HEAD_DIM + 8 * o + 8, :] for g in range(G_NSA) for r in x_refs]
            by_d = jnp.swapaxes(jnp.stack(rows), 0, 1)
            for q4 in (2 * o, 2 * o + 1):
                x = [by_d[4 * (q4 - 2 * o) + j] for j in range(4)]
                sw = [pltpu.roll(v, CMP_BLOCK, 1) for v in x]
                a0 = jnp.concatenate([jnp.where(lo, x[0], sw[1]), jnp.where(lo, x[2], sw[3])], axis=1)
                a1 = jnp.concatenate([jnp.where(lo, sw[0], x[1]), jnp.where(lo, sw[2], x[3])], axis=1)
                a = jnp.concatenate([a0, a1], axis=0) + pe_ref[kv, q4:q4 + 1, :]
                h = h + _dot(a.astype(BF16), w1[256 * q4:256 * (q4 + 1), :])
        hs.append(h)
    gk = jax.nn.gelu(hs[0]).astype(BF16)
    gv = jax.nn.gelu(hs[1]).astype(BF16)

    def second(gx, w2_ref):
        outs = []
        for par in range(2):
            r0 = par * n_sl
            outs.append(_dot(gx[r0:r0 + g_pages], w2_ref[0]) + _dot(gx[r0 + g_pages:r0 + n_sl], w2_ref[1]))
        return jnp.concatenate(outs, axis=0)

    kc = _rope(_head_norm(second(gk, w2k_ref), s64_ref[...], gkc_ref[...]), cos_ref[...], sin_ref[...])
    vc = second(gv, w2v_ref)
    for par in range(2):
        kc_ref[0, 0, par] = kc[par * g_pages:(par + 1) * g_pages]
        vc_ref[0, 0, par] = vc[par * g_pages:(par + 1) * g_pages]


def _compress_paged(cache5, layer, page_table, lw, cos, sin, g_pages):
    db, n_pages = page_table.shape
    page = cache5.shape[4]
    ns = n_pages // g_pages
    nmap = lambda n: (lambda b, s, pt: (0,) * n)

    def page_spec(i):
        return pl.BlockSpec((None, None, 2, W_GRP, page),
                            lambda b, s, pt, i=i: (layer, pt[b, s * g_pages + i], 0, 0, 0))

    out_spec = pl.BlockSpec((1, 1, 2, g_pages, LANES), lambda b, s, pt: (b, s, 0, 0, 0))
    grid_spec = pltpu.PrefetchScalarGridSpec(
        num_scalar_prefetch=1,
        grid=(db, ns),
        in_specs=[page_spec(i) for i in range(g_pages)] + _compress_weight_specs(nmap) + [
            pl.BlockSpec((2 * g_pages, LANES), lambda b, s, pt: (s, 0)),
            pl.BlockSpec((2 * g_pages, LANES), lambda b, s, pt: (s, 0)),
            pl.BlockSpec((256, 256), nmap(2)),
        ],
        out_specs=(out_spec, out_spec),
    )
    shp = jax.ShapeDtypeStruct((db, ns, 2, g_pages, LANES), F32)
    kc, vc = pl.pallas_call(
        functools.partial(_compress_paged_kernel, g_pages=g_pages),
        grid_spec=grid_spec,
        out_shape=(shp, shp),
        compiler_params=_cparams("arbitrary", "arbitrary"),
        name="nsa_compress_paged",
    )(page_table, *([cache5] * g_pages), lw["pe4t"], lw["w1kt"], lw["w1vt"], lw["w2k"], lw["w2v"], lw["gkc"],
      cos, sin, lw["s64"])
    nb_past = n_pages * page // CMP_BLOCK
    order = lambda a: a.transpose(0, 1, 3, 2, 4).reshape(db, nb_past, LANES)
    return order(kc), order(vc)


def _fox_decode_kernel(pt_ref, *refs, g_pages, n_pages):
    kv_refs = refs[:g_pages]
    lf_ref, q_ref, knew_ref, lfn_ref, o_ref, m_s, l_s, acc_s, carry_s, nc_s = refs[g_pages:]
    step = pl.program_id(1)
    bi = pl.program_id(0)
    tqn = q_ref.shape[1]
    rows = H_FOX * tqn
    q4 = _stack_heads(q_ref[0], H_FOX)
    tqv = _iota((rows, 1), 0) & (tqn - 1)

    def expand_heads(x):
        return jnp.concatenate([jnp.broadcast_to(x[h:h + 1, :], (tqn, x.shape[1])) for h in range(H_FOX)], axis=0)

    @pl.when(step == 0)
    def _():
        lfn = lfn_ref[0]
        lane8 = _iota(lfn.shape, 1)
        cum = lfn
        s = 1
        while s < tqn:
            cum = cum + jnp.where(lane8 >= s, pltpu.roll(cum, s, 1), 0.0)
            s *= 2
        e = expand_heads(cum)
        lane = _iota(e.shape, 1)
        nc = jnp.sum(jnp.where(lane == tqv, e, 0.0), axis=1, keepdims=True)
        nc_s[...] = nc
        carry_s[...] = jnp.zeros(carry_s.shape, F32)
        m_s[...] = jnp.full(m_s.shape, -jnp.inf, F32)
        l_s[...] = jnp.zeros(l_s.shape, F32)
        acc_s[...] = jnp.zeros(acc_s.shape, F32)
        sc = _dot(q4, knew_ref[0, 0].astype(BF16)) + nc - e
        _softmax_update(sc, lane <= tqv, m_s, l_s, acc_s, knew_ref[0, 1].astype(BF16))

    kt = jnp.concatenate([r[0].astype(BF16) for r in kv_refs], axis=1)
    vt = jnp.concatenate([r[1].astype(BF16) for r in kv_refs], axis=1)
    carry = carry_s[...]
    xs = []
    for i in range(g_pages):
        pg = pt_ref[bi, n_pages - 1 - (step * g_pages + i)]
        pair = lf_ref[pg >> 1]
        lf = jnp.where((pg & 1) == 1, pair[H_FOX:], pair[:H_FOX])
        excl, tot = _suffix_excl_lanes(lf)
        xs.append(excl + carry)
        carry = carry + tot
    carry_s[...] = carry
    bias = expand_heads(jnp.concatenate(xs, axis=1))
    _softmax_update(_dot(q4, kt) + nc_s[...] + bias, None, m_s, l_s, acc_s, vt)

    @pl.when(step == pl.num_programs(1) - 1)
    def _():
        o_ref[0] = _unstack_heads(acc_s[...] / jnp.maximum(l_s[...], 1e-30), H_FOX, tqn)


def _fox_decode(q3, cache5, logf4, layer, page_table, knew, lfn, g_pages):
    db, n_pages = page_table.shape
    page = cache5.shape[4]
    tqn = q3.shape[1]
    ns = n_pages // g_pages
    rows = H_FOX * tqn

    def pg(b, s, pt, i):
        return pt[b, n_pages - 1 - (s * g_pages + i)]

    kv_specs = [pl.BlockSpec((None, None, 2, W_FOX, page), lambda b, s, pt, i=i: (layer, pg(b, s, pt, i), 0, 0, 0))
                for i in range(g_pages)]
    n_pool = logf4.shape[1]
    lf_pairs = logf4.reshape(logf4.shape[0], n_pool // 2, 2 * H_FOX, page)
    lf_spec = pl.BlockSpec((None, n_pool // 2, 2 * H_FOX, page), lambda b, s, pt: (layer, 0, 0, 0))
    grid_spec = pltpu.PrefetchScalarGridSpec(
        num_scalar_prefetch=1,
        grid=(db, ns),
        in_specs=kv_specs + [lf_spec] + [
            pl.BlockSpec((1, tqn, W_FOX), lambda b, s, pt: (b, 0, 0)),
            pl.BlockSpec((1, 2, W_FOX, page), lambda b, s, pt: (b, 0, 0, 0)),
            pl.BlockSpec((1, 8, LANES), lambda b, s, pt: (b, 0, 0)),
        ],
        out_specs=pl.BlockSpec((1, tqn, W_FOX), lambda b, s, pt: (b, 0, 0)),
        scratch_shapes=[
            pltpu.VMEM((rows, 1), F32), pltpu.VMEM((rows, 1), F32), pltpu.VMEM((rows, W_FOX), F32),
            pltpu.VMEM((H_FOX, 1), F32), pltpu.VMEM((rows, 1), F32),
        ],
    )
    return pl.pallas_call(
        functools.partial(_fox_decode_kernel, g_pages=g_pages, n_pages=n_pages),
        grid_spec=grid_spec,
        out_shape=jax.ShapeDtypeStruct((db, tqn, W_FOX), F32),
        compiler_params=_cparams("arbitrary", "arbitrary"),
        name="fox_decode",
    )(page_table, *([cache5] * g_pages), lf_pairs, q3, knew, lfn)


def _stick_decode_kernel(pt_ref, *refs, g_pages):
    kv_refs = refs[:g_pages]
    q_ref, knew_ref, o_ref, r_s, acc_s = refs[g_pages:]
    step = pl.program_id(1)
    tqn = q_ref.shape[1]
    rows = H_STK * tqn
    q4 = _stack_heads(q_ref[0], H_STK)
    tqv = _iota((rows, 1), 0) & (tqn - 1)

    tri = jnp.where(_iota((LANES, LANES), 0) > _iota((LANES, LANES), 1), 1.0, 0.0).astype(BF16)

    def sweep(z, mask, r_col):
        ls = _log_sigmoid(z)
        lk = ls - z
        if mask is not None:
            lk = jnp.where(mask, lk, 0.0)
        n_ch = z.shape[1] // LANES
        chunks = [lk[:, i * LANES:(i + 1) * LANES] for i in range(n_ch)]
        st = chunks[0] if n_ch == 1 else jnp.concatenate(chunks, axis=0)
        hi = st.astype(BF16)
        lo = (st - hi.astype(F32)).astype(BF16)
        ex = _dot(hi, tri) + _dot(lo, tri)
        rcs = []
        for i in range(n_ch):
            rcs.append(ex[i * rows:(i + 1) * rows] + r_col)
            r_col = r_col + jnp.sum(chunks[i], axis=1, keepdims=True)
        rc = rcs[0] if n_ch == 1 else jnp.concatenate(rcs, axis=1)
        a = jnp.exp(ls + rc)
        if mask is not None:
            a = jnp.where(mask, a, 0.0)
        return a, r_col

    @pl.when(step == 0)
    def _():
        z = _dot(q4, knew_ref[0, 0].astype(BF16))
        lane = _iota(z.shape, 1)
        a, r_col = sweep(z, lane < tqv, jnp.zeros((rows, 1), F32))
        r_s[...] = r_col
        acc_s[...] = _dot_nt(a.astype(BF16), knew_ref[0, 1].astype(BF16))

    kt = jnp.concatenate([r[0].astype(BF16) for r in kv_refs], axis=1)
    vt = jnp.concatenate([r[1].astype(BF16) for r in kv_refs], axis=1)
    a, r_col = sweep(_dot(q4, kt), None, r_s[...])
    r_s[...] = r_col
    acc_s[...] = acc_s[...] + _dot_nt(a.astype(BF16), vt)

    @pl.when(step == pl.num_programs(1) - 1)
    def _():
        o_ref[0] = _unstack_heads(acc_s[...], H_STK, tqn)


def _stick_decode(q3, cache5, layer, page_table, knew, g_pages):
    db, n_pages = page_table.shape
    page = cache5.shape[4]
    tqn = q3.shape[1]
    ns = n_pages // g_pages
    rows = H_STK * tqn
    kv_specs = [pl.BlockSpec((None, None, 2, W_STK, page),
                             lambda b, s, pt, i=i: (layer, pt[b, n_pages - 1 - (s * g_pages + i)], 0, 0, 0))
                for i in range(g_pages)]
    grid_spec = pltpu.PrefetchScalarGridSpec(
        num_scalar_prefetch=1,
        grid=(db, ns),
        in_specs=kv_specs + [
            pl.BlockSpec((1, tqn, W_STK), lambda b, s, pt: (b, 0, 0)),
            pl.BlockSpec((1, 2, W_STK, page), lambda b, s, pt: (b, 0, 0, 0)),
        ],
        out_specs=pl.BlockSpec((1, tqn, W_STK), lambda b, s, pt: (b, 0, 0)),
        scratch_shapes=[pltpu.VMEM((rows, 1), F32), pltpu.VMEM((rows, W_STK), F32)],
    )
    return pl.pallas_call(
        functools.partial(_stick_decode_kernel, g_pages=g_pages),
        grid_spec=grid_spec,
        out_shape=jax.ShapeDtypeStruct((db, tqn, W_STK), F32),
        compiler_params=_cparams("arbitrary", "arbitrary"),
        name="stick_decode",
    )(page_table, *([cache5] * g_pages), q3, knew)


def _nsa_sample_small_kernel(q_ref, kc_ref, vc_ref, win_ref, oc_ref, ow_ref, selm_ref, *, nb, pos0, win_pos0):
    q = q_ref[0]
    tqn = q.shape[0]
    nbp = kc_ref.shape[1]
    qs = _stack_nsa(q)
    rows = qs.shape[0]
    tpos = pos0 + (_iota((rows, 1), 0) & (tqn - 1))
    kc = kc_ref[0].astype(BF16)
    vc = vc_ref[0].astype(BF16)
    n_idx = _iota((rows, nbp), 1)
    cmask = ((n_idx << CMP_SHIFT) + (CMP_BLOCK - 1) <= tpos) & (n_idx < nb)
    pc = _masked_softmax_rows(_dot_nt(qs, kc), cmask)
    oc_ref[0] = _unstack_nsa(_dot(pc.astype(BF16), vc), tqn)

    imp = []
    for g in range(G_NSA):
        acc = jnp.zeros((tqn, nbp), F32)
        for c in range(R_NSA):
            blk = 2 * c + g
            acc = acc + pc[blk * tqn:(blk + 1) * tqn]
        imp.append(acc)
    imp = jnp.concatenate(imp, axis=0)
    tp2 = pos0 + (_iota((G_NSA * tqn, 1), 0) & (tqn - 1))
    n2 = _iota(imp.shape, 1)
    visible = ((n2 << CMP_SHIFT) <= tp2) & (n2 < nb)
    cur = tp2 >> CMP_SHIFT
    forced = (n2 == 0) | (n2 == cur) | (n2 == cur - 1)
    score = jnp.where(visible, imp + FORCE_BONUS * jnp.where(forced, 1.0, 0.0), -jnp.inf)
    selected = _block_select(score, visible, n2, nb, min(TOP_N, nb), nbp)
    selm_ref[0] = jnp.where(selected, 1.0, 0.0)

    kw = win_ref[0, :, :W_GRP].astype(BF16)
    vw = win_ref[0, :, W_GRP:].astype(BF16)
    kpos = win_pos0 + _iota((1, kw.shape[0]), 1)
    pw = _masked_softmax_rows(_dot_nt(qs, kw), (kpos <= tpos) & (kpos > tpos - WINDOW) & (kpos >= win_pos0))
    ow_ref[0] = _unstack_nsa(_dot(pw.astype(BF16), vw), tqn)


def _nsa_sample_small(q3, kc, vc, win, nb, pos0, win_pos0):
    db, tqn, _ = q3.shape
    nbp = kc.shape[1]
    wk = win.shape[1]
    return pl.pallas_call(
        functools.partial(_nsa_sample_small_kernel, nb=nb, pos0=pos0, win_pos0=win_pos0),
        grid=(db,),
        in_specs=[
            pl.BlockSpec((1, tqn, W_NSA), lambda b: (b, 0, 0)),
            pl.BlockSpec((1, nbp, LANES), lambda b: (b, 0, 0)),
            pl.BlockSpec((1, nbp, LANES), lambda b: (b, 0, 0)),
            pl.BlockSpec((1, wk, 2 * W_GRP), lambda b: (b, 0, 0)),
        ],
        out_specs=(pl.BlockSpec((1, tqn, W_NSA), lambda b: (b, 0, 0)),
                   pl.BlockSpec((1, tqn, W_NSA), lambda b: (b, 0, 0)),
                   pl.BlockSpec((1, G_NSA * tqn, nbp), lambda b: (b, 0, 0))),
        out_shape=(jax.ShapeDtypeStruct((db, tqn, W_NSA), F32), jax.ShapeDtypeStruct((db, tqn, W_NSA), F32),
                   jax.ShapeDtypeStruct((db, G_NSA * tqn, nbp), F32)),
        compiler_params=_cparams("arbitrary"),
        name="nsa_sample_small",
    )(q3, kc, vc, win)


def _sel_decode_kernel(pt_ref, *refs, g_pages, nb_new):
    kv_refs = refs[:g_pages]
    (q_ref, knew_ref, msel_ref, mnew_ref, ee_ref, oc_ref, ow_ref, misc_ref, eg_ref,
     o_ref, m_s, l_s, acc_s) = refs[g_pages:]
    step = pl.program_id(1)
    tqn = q_ref.shape[1]
    qs = _stack_nsa(q_ref[0])
    rows = qs.shape[0]
    tqv = _iota((rows, 1), 0) & (tqn - 1)

    @pl.when(step == 0)
    def _():
        m_s[...] = jnp.full(m_s.shape, -jnp.inf, F32)
        l_s[...] = jnp.zeros(l_s.shape, F32)
        acc_s[...] = jnp.zeros(acc_s.shape, F32)
        s = _dot(qs, knew_ref[0, 0].astype(BF16))
        lane = _iota(s.shape, 1)
        mn = mnew_ref[0][:, nb_new:nb_new + 1] > 0.5
        mn = jnp.concatenate([mn] * R_NSA, axis=0)
        _softmax_update(s, mn & (lane <= tqv), m_s, l_s, acc_s, knew_ref[0, 1].astype(BF16))

    kt = jnp.concatenate([r[0].astype(BF16) for r in kv_refs], axis=1)
    vt = jnp.concatenate([r[1].astype(BF16) for r in kv_refs], axis=1)
    s = _dot(qs, kt)
    me = _dot(msel_ref[0, 0].astype(BF16), ee_ref[...]) > 0.5
    _softmax_update(s, jnp.concatenate([me] * R_NSA, axis=0), m_s, l_s, acc_s, vt)

    @pl.when(step == pl.num_programs(1) - 1)
    def _():
        o_s = _unstack_nsa(acc_s[...] / jnp.maximum(l_s[...], 1e-30), tqn)
        g_c, g_s, g_w = _gate_expand(misc_ref[0], eg_ref)
        o_ref[0] = g_c * oc_ref[0] + g_s * o_s + g_w * ow_ref[0]


def _sel_decode(q3, cache5, layer, page_table, knew, selm, o_c, o_w, misc3, consts, g_pages):
    db, n_pages = page_table.shape
    page = cache5.shape[4]
    tqn = q3.shape[1]
    ns = n_pages // g_pages
    rows = H_NSA * tqn
    nbp = selm.shape[2]
    bps = g_pages * page // CMP_BLOCK
    nb_past = n_pages * page // CMP_BLOCK
    msel = selm[:, :, :nb_past].reshape(db, G_NSA * tqn, ns, bps).transpose(0, 2, 1, 3)
    kv_specs = [pl.BlockSpec((None, None, 2, W_GRP, page),
                             lambda b, s, pt, i=i: (layer, pt[b, s * g_pages + i], 0, 0, 0))
                for i in range(g_pages)]
    per_b = lambda b, s, pt: (b, 0, 0)
    grid_spec = pltpu.PrefetchScalarGridSpec(
        num_scalar_prefetch=1,
        grid=(db, ns),
        in_specs=kv_specs + [
            pl.BlockSpec((1, tqn, W_NSA), per_b),
            pl.BlockSpec((1, 2, W_GRP, page), lambda b, s, pt: (b, 0, 0, 0)),
            pl.BlockSpec((1, 1, G_NSA * tqn, bps), lambda b, s, pt: (b, s, 0, 0)),
            pl.BlockSpec((1, G_NSA * tqn, nbp), per_b),
            pl.BlockSpec((bps, g_pages * page), lambda b, s, pt: (0, 0)),
            pl.BlockSpec((1, tqn, W_NSA), per_b),
            pl.BlockSpec((1, tqn, W_NSA), per_b),
            pl.BlockSpec((1, tqn, LANES), per_b),
            pl.BlockSpec((3, LANES, W_NSA), lambda b, s, pt: (0, 0, 0)),
        ],
        out_specs=pl.BlockSpec((1, tqn, W_NSA), per_b),
        scratch_shapes=[pltpu.VMEM((rows, 1), F32), pltpu.VMEM((rows, 1), F32), pltpu.VMEM((rows, LANES), F32)],
    )
    return pl.pallas_call(
        functools.partial(_sel_decode_kernel, g_pages=g_pages, nb_new=nb_past),
        grid_spec=grid_spec,
        out_shape=jax.ShapeDtypeStruct((db, tqn, W_NSA), F32),
        compiler_params=_cparams("arbitrary", "arbitrary"),
        name="nsa_sel_decode",
    )(page_table, *([cache5] * g_pages), q3, knew, msel, selm, consts["ee_decode"], o_c, o_w, misc3,
      consts["egate"])


def _mix_ffn_kernel(x_ref, of_ref, on_ref, os_ref, gmix_ref, wo_ref, gffn_ref, wg_ref, wu_ref, wd_ref, o_ref,
                    x1_s, xn_s, acc_s):
    j = pl.program_id(1)

    def rms(y, g):
        return y * lax.rsqrt(jnp.mean(y * y, axis=-1, keepdims=True) + EPS) * g

    @pl.when(j == 0)
    def _():
        o = jnp.concatenate([
            rms(of_ref[...], gmix_ref[:, :W_FOX]),
            rms(on_ref[...], gmix_ref[:, W_FOX:W_FOX + W_NSA]),
            rms(os_ref[...], gmix_ref[:, W_FOX + W_NSA:]),
        ], axis=1).astype(BF16)
        x1 = x_ref[...] + _dot(o, wo_ref[...])
        x1_s[...] = x1
        xn_s[...] = rms(x1, gffn_ref[...]).astype(BF16)
        acc_s[...] = jnp.zeros(acc_s.shape, F32)

    xn = xn_s[...]
    hg = _dot(xn, wg_ref[...])
    hu = _dot(xn, wu_ref[...])
    h = (hg * (1.0 / (1.0 + jnp.exp(-hg))) * hu).astype(BF16)
    acc_s[...] = acc_s[...] + _dot(h, wd_ref[...])

    @pl.when(j == pl.num_programs(1) - 1)
    def _():
        o_ref[...] = x1_s[...] + acc_s[...]


def _mix_ffn(x2d, o_fox, o_nsa, o_stk, lw, tn, tf):
    n, d = x2d.shape
    dff = lw["w_gate"].shape[1]
    row = lambda i, j: (i, 0)
    const = lambda i, j: (0, 0)
    return pl.pallas_call(
        _mix_ffn_kernel,
        grid=(n // tn, dff // tf),
        in_specs=[
            pl.BlockSpec((tn, d), row),
            pl.BlockSpec((tn, W_FOX), row),
            pl.BlockSpec((tn, W_NSA), row),
            pl.BlockSpec((tn, W_STK), row),
            pl.BlockSpec((1, d), const),
            pl.BlockSpec((d, d), const),
            pl.BlockSpec((1, d), const),
            pl.BlockSpec((d, tf), lambda i, j: (0, j)),
            pl.BlockSpec((d, tf), lambda i, j: (0, j)),
            pl.BlockSpec((tf, d), lambda i, j: (j, 0)),
        ],
        out_specs=pl.BlockSpec((tn, d), row),
        out_shape=jax.ShapeDtypeStruct((n, d), F32),
        scratch_shapes=[pltpu.VMEM((tn, d), F32), pltpu.VMEM((tn, d), BF16), pltpu.VMEM((tn, d), F32)],
        compiler_params=_cparams("arbitrary", "arbitrary"),
        name="mix_ffn",
    )(x2d, o_fox, o_nsa, o_stk, lw["g_mix"], lw["w_o"], lw["g_ffn"], lw["w_gate"], lw["w_up"], lw["w_down"])


def _nsa_lane_perm():
    perm = []
    for h in NSA_HEAD_ORDER:
        perm.extend(range(h * HEAD_DIM, (h + 1) * HEAD_DIM))
    return np.asarray(perm, np.int32)


def _layer_weights(l, w):
    d = w["w_in"].shape[1]
    perm = _nsa_lane_perm()
    wt = jnp.swapaxes(w["w_in"], 1, 2)[l]
    o_fq, o_fk, o_ff, o_nq = 0, W_FOX, 3 * W_FOX, 3 * W_FOX + H_FOX
    o_kc = o_nq + W_NSA
    o_ng = o_kc + 6 * W_GRP
    o_sq = o_ng + 3 * H_NSA
    w_perm = jnp.concatenate(
        [wt[o_fq:o_fq + W_FOX], wt[o_fk:o_fk + 2 * W_FOX]]
        + [wt[o_nq + h * HEAD_DIM:o_nq + (h + 1) * HEAD_DIM] for h in NSA_HEAD_ORDER]
        + [wt[o_kc:o_kc + 6 * W_GRP], wt[o_sq:o_sq + W_STK], wt[o_sq + W_STK:o_sq + 3 * W_STK],
           wt[o_ff:o_ff + H_FOX], wt[o_ng:o_ng + 3 * H_NSA], jnp.zeros((LANES - N_MISC, d), F32)],
        axis=0).astype(BF16)
    bias = jnp.concatenate([w["b_fox_f"][l], w["b_nsa_gate"][l], jnp.zeros((LANES - N_MISC,), F32)])[None, :]

    def tiled(g, width):
        return jnp.pad(jnp.tile(g, width // HEAD_DIM), (0, W_NSA - width))

    gains = jnp.stack([tiled(w["fox_gq"][l], W_FOX), tiled(w["fox_gk"][l], W_FOX), tiled(w["nsa_gq"][l], W_NSA),
                       tiled(w["nsa_gks"][l], W_GRP), tiled(w["nsa_gkw"][l], W_GRP)]
                      + [jnp.zeros((W_NSA,), F32)] * 3)
    s64 = jnp.asarray(np.kron(np.eye(4, dtype=np.float32), np.full((64, 64), 1.0 / 64, np.float32)), BF16)

    def w2pad(w2):
        z = jnp.zeros_like(w2)
        return jnp.stack([jnp.concatenate([w2, z], axis=1), jnp.concatenate([z, w2], axis=1)]).astype(BF16)

    def by_feature(w1):
        return w1.reshape(CMP_BLOCK, HEAD_DIM, CMP_HIDDEN).transpose(1, 0, 2).reshape(CMP_BLOCK * HEAD_DIM, CMP_HIDDEN).astype(BF16)

    g_mix = w["g_mix"][l]
    g_mix = jnp.concatenate([g_mix[:W_FOX], g_mix[W_FOX:W_FOX + W_NSA][perm], g_mix[W_FOX + W_NSA:]])[None, :]
    w_o = w["w_o"][l]
    w_o = jnp.concatenate([w_o[:W_FOX], w_o[W_FOX:W_FOX + W_NSA][perm], w_o[W_FOX + W_NSA:]], axis=0).astype(BF16)
    return {
        "g_attn": w["g_attn"][l][None, :], "w_in": w_perm, "bias_misc": bias, "gains": gains, "s64": s64,
        "pe4": jnp.stack([w["nsa_pe_k"][l].reshape(CMP_BLOCK // 4, 256), w["nsa_pe_v"][l].reshape(CMP_BLOCK // 4, 256)]),
        "w1k": w["nsa_w1k"][l].astype(BF16), "w1v": w["nsa_w1v"][l].astype(BF16),
        "pe4t": jnp.stack([w["nsa_pe_k"][l].T.reshape(HEAD_DIM // 4, 256), w["nsa_pe_v"][l].T.reshape(HEAD_DIM // 4, 256)]),
        "w1kt": by_feature(w["nsa_w1k"][l]), "w1vt": by_feature(w["nsa_w1v"][l]),
        "w2k": w2pad(w["nsa_w2k"][l]), "w2v": w2pad(w["nsa_w2v"][l]),
        "gkc": jnp.tile(w["nsa_gkc"][l], 2)[None, :],
        "g_mix": g_mix, "w_o": w_o, "g_ffn": w["g_ffn"][l][None, :],
        "w_gate": w["w_gate"][l].astype(BF16), "w_up": w["w_up"][l].astype(BF16), "w_down": w["w_down"][l].astype(BF16),
    }


def _constants(t, tk, g_sel, page):
    perm = _nsa_lane_perm()
    egate = np.zeros((3, LANES, W_NSA), np.float32)
    for lane in range(W_NSA):
        h = int(perm[lane]) // HEAD_DIM
        for j in range(3):
            egate[j, H_FOX + 3 * h + j, lane] = 1.0
    nb = t // CMP_BLOCK
    eb = np.zeros((G_NSA, LANES, t), np.float32)
    for g in range(G_NSA):
        for n in range(nb):
            eb[g, g * HEAD_DIM + n, n * CMP_BLOCK:(n + 1) * CMP_BLOCK] = 1.0
    bps = g_sel * page // CMP_BLOCK
    ee = np.zeros((bps, g_sel * page), np.float32)
    for n in range(bps):
        ee[n, n * CMP_BLOCK:(n + 1) * CMP_BLOCK] = 1.0
    eb = eb.reshape(G_NSA, LANES, t // tk, tk).transpose(0, 2, 1, 3)
    return {"egate": jnp.asarray(egate, BF16), "eb_prompt": jnp.asarray(eb, BF16), "ee_decode": jnp.asarray(ee, BF16)}


def _nsa_tk(t):
    return 2 * LANES if t % (2 * LANES) == 0 else LANES


def _pick(n, prefs):
    for p in prefs:
        if n % p == 0:
            return p
    return n


def _prompt_layer(x3, lw, tabs, consts):
    b, t, d = x3.shape
    n = b * t
    tn = _pick(t, (256, 128))
    (qf, fkv, qn, cmpr, sel, win, qs, skv, misc) = _proj(x3.reshape(n, d), lw, tabs["cos_p"], tabs["sin_p"], t // tn, tn)
    tq = _pick(t, (256, 128))
    tk = tq
    misc3 = misc.reshape(b, t, LANES)
    ccol, crow = _cumsum(misc3, tk)
    o_fox = _fox_prompt(qf.reshape(b, t, W_FOX), fkv.reshape(b, t, 2 * W_FOX), ccol, crow, tq, tk)
    o_stk = _stick_prompt(qs.reshape(b, t, W_STK), skv.reshape(b, t, 2 * W_STK), tq, tk)
    kc, vc = _compress_contig(cmpr.reshape(b, t, 2 * W_GRP), lw, tabs["cos_bp"], tabs["sin_bp"])
    o_nsa = _nsa_prompt(qn.reshape(b, t, W_NSA), kc, vc, sel.reshape(b, t, 2 * W_GRP), win.reshape(b, t, 2 * W_GRP),
                        misc3, consts, LANES, _nsa_tk(t))
    tnf = _pick(n, (512, 256, 128))
    x_out = _mix_ffn(x3.reshape(n, d), o_fox.reshape(n, W_FOX), o_nsa.reshape(n, W_NSA), o_stk.reshape(n, W_STK),
                     lw, tnf, _pick(lw["w_gate"].shape[1], (1408, 1024, 512, 256, 128)))
    n_win = min(WINDOW, t)
    rows = (fkv.reshape(b, t, 2, H_FOX, HEAD_DIM), misc3[:, :, :H_FOX], cmpr.reshape(b, t, 2, G_NSA, HEAD_DIM),
            sel.reshape(b, t, 2, G_NSA, HEAD_DIM), win.reshape(b, t, 2, G_NSA, HEAD_DIM)[:, t - n_win:],
            skv.reshape(b, t, 2, H_STK, HEAD_DIM))
    return x_out.reshape(b, t, d), rows


def _sample_layer(x3, l, lw, tabs, consts, caches, page_table, g_pages):
    db, tqn, d = x3.shape
    n = db * tqn
    n_pages = page_table.shape[1]
    page = caches["fox"].shape[4]
    past = n_pages * page
    (qf, fkv, qn, cmpr, sel, win, qs, skv, misc) = _proj(x3.reshape(n, d), lw, tabs["cos_s"], tabs["sin_s"], 1, n)
    misc3 = misc.reshape(db, tqn, LANES)

    def new_page(rows3):
        w_ = rows3.shape[2] // 2
        r = rows3.reshape(db, tqn, 2, w_).transpose(0, 2, 3, 1)
        return jnp.pad(r, ((0, 0), (0, 0), (0, 0), (0, page - tqn)))

    fkv3 = fkv.reshape(db, tqn, 2 * W_FOX)
    lfn = jnp.pad(jnp.swapaxes(misc3[:, :, :H_FOX], 1, 2), ((0, 0), (0, 8 - H_FOX), (0, LANES - tqn)))
    o_fox = _fox_decode(qf.reshape(db, tqn, W_FOX), caches["fox"], caches["logf"], l, page_table,
                        new_page(fkv3), lfn, g_pages["fox"])
    skv3 = skv.reshape(db, tqn, 2 * W_STK)
    o_stk = _stick_decode(qs.reshape(db, tqn, W_STK), caches["stk"], l, page_table, new_page(skv3), g_pages["stk"])

    cmp3 = cmpr.reshape(db, tqn, 2 * W_GRP)
    kc_p, vc_p = _compress_paged(caches["cmp"], l, page_table, lw, tabs["cos_bs"], tabs["sin_bs"], g_pages["cmp"])
    new_blk = jnp.pad(cmp3, ((0, 0), (0, CMP_BLOCK - tqn), (0, 0))).reshape(1, db * CMP_BLOCK, 2 * W_GRP)
    kc_n, vc_n = _compress_contig(new_blk, lw, tabs["cos_bn"], tabs["sin_bn"])
    nb = past // CMP_BLOCK + 1
    nbp = -(-nb // LANES) * LANES
    cat = lambda a, c: jnp.pad(jnp.concatenate([a, c.reshape(db, 1, LANES)], axis=1), ((0, 0), (0, nbp - nb), (0, 0)))
    kc = cat(kc_p, kc_n)
    vc = cat(vc_p, vc_n)
    win3 = win.reshape(db, tqn, 2 * W_GRP)
    win_all = jnp.concatenate([caches["win"][l], win3], axis=1)
    wk = win_all.shape[1]
    wkp = -(-wk // LANES) * LANES
    q3 = qn.reshape(db, tqn, W_NSA)
    o_c, o_w, selm = _nsa_sample_small(q3, kc, vc, jnp.pad(win_all, ((0, 0), (0, wkp - wk), (0, 0))),
                                       nb, past, past + tqn - wk)
    sel3 = sel.reshape(db, tqn, 2 * W_GRP)
    o_nsa = _sel_decode(q3, caches["sel"], l, page_table, new_page(sel3), selm, o_c, o_w, misc3, consts, g_pages["sel"])

    x_out = _mix_ffn(x3.reshape(n, d), o_fox.reshape(n, W_FOX), o_nsa.reshape(n, W_NSA), o_stk.reshape(n, W_STK),
                     lw, n, _pick(lw["w_gate"].shape[1], (1408, 1024, 512, 256, 128)))
    n_win = caches["win"].shape[2]
    rows = (fkv.reshape(db, tqn, 2, H_FOX, HEAD_DIM), misc3[:, :, :H_FOX], cmpr.reshape(db, tqn, 2, G_NSA, HEAD_DIM),
            sel.reshape(db, tqn, 2, G_NSA, HEAD_DIM), win_all[:, wk - n_win:].reshape(db, n_win, 2, G_NSA, HEAD_DIM),
            skv.reshape(db, tqn, 2, H_STK, HEAD_DIM))
    return x_out.reshape(db, tqn, d), rows


def kernel(x_prompt, x_sample, cache_fox_kv, cache_fox_logf, cache_nsa_cmp_kv, cache_nsa_sel_kv, state_nsa_win_kv, cache_stk_kv, page_table, g_attn, w_in, b_fox_f, b_nsa_gate, fox_gq, fox_gk, nsa_gq, nsa_gkc, nsa_gks, nsa_gkw, nsa_pe_k, nsa_w1k, nsa_w2k, nsa_pe_v, nsa_w1v, nsa_w2v, g_mix, w_o, g_ffn, w_gate, w_up, w_down):
    weights = dict(g_attn=g_attn, w_in=w_in, b_fox_f=b_fox_f, b_nsa_gate=b_nsa_gate, fox_gq=fox_gq, fox_gk=fox_gk,
                   nsa_gq=nsa_gq, nsa_gkc=nsa_gkc, nsa_gks=nsa_gks, nsa_gkw=nsa_gkw, nsa_pe_k=nsa_pe_k,
                   nsa_w1k=nsa_w1k, nsa_w2k=nsa_w2k, nsa_pe_v=nsa_pe_v, nsa_w1v=nsa_w1v, nsa_w2v=nsa_w2v,
                   g_mix=g_mix, w_o=w_o, g_ffn=g_ffn, w_gate=w_gate, w_up=w_up, w_down=w_down)
    depth = w_in.shape[0]
    b, t, d = x_prompt.shape
    db, tqn, _ = x_sample.shape
    n_pool, page = cache_fox_kv.shape[1], cache_fox_kv.shape[2]
    n_pages = page_table.shape[1]
    past = n_pages * page
    assert t % LANES == 0 and t // CMP_BLOCK <= HEAD_DIM and tqn == 8 and page == LANES
    n_win_s = state_nsa_win_kv.shape[2]

    def paged(cache, w_):
        return jnp.transpose(cache, (0, 1, 3, 4, 5, 2)).reshape(depth, n_pool, 2, w_, page)

    caches = {
        "fox": paged(cache_fox_kv, W_FOX),
        "logf": jnp.swapaxes(cache_fox_logf, 2, 3),
        "cmp": paged(cache_nsa_cmp_kv, W_GRP),
        "sel": paged(cache_nsa_sel_kv, W_GRP),
        "stk": paged(cache_stk_kv, W_STK),
        "win": state_nsa_win_kv.reshape(depth, db, n_win_s, 2 * W_GRP),
    }
    g_pages = {k: _pick(n_pages, (32, 16, 8, 4, 2, 1)) for k in ("fox", "stk", "sel")}
    g_pages["cmp"] = _pick(n_pages, (32, 16, 8, 4, 2, 1))
    consts = _constants(t, _nsa_tk(t), g_pages["sel"], page)

    nbp_blocks = t // CMP_BLOCK
    nbs_blocks = past // CMP_BLOCK
    bend = lambda n0, cnt: (n0 + jnp.arange(cnt, dtype=jnp.int32)) * CMP_BLOCK + (CMP_BLOCK - 1)
    pos_all = jnp.concatenate([
        jnp.arange(t, dtype=jnp.int32),
        jnp.tile(past + jnp.arange(tqn, dtype=jnp.int32), db),
        bend(0, nbp_blocks), bend(0, nbs_blocks).reshape(-1, g_pages["cmp"], 2).transpose(0, 2, 1).reshape(-1),
        jnp.tile(bend(nbs_blocks, 1), db),
    ])
    pad = (-pos_all.shape[0]) % 8
    cos_all, sin_all = _rope_tables(jnp.pad(pos_all, (0, pad)))
    offs = np.cumsum([0, t, db * tqn, nbp_blocks, nbs_blocks, db])
    names = ("p", "s", "bp", "bs", "bn")
    tabs = {}
    for i, nm in enumerate(names):
        tabs["cos_" + nm] = cos_all[offs[i]:offs[i + 1]]
        tabs["sin_" + nm] = sin_all[offs[i]:offs[i + 1]]

    xp, xs = x_prompt, x_sample
    rows_p, rows_s = [], []
    for l in range(depth):
        lw = _layer_weights(l, weights)
        xp, rp = _prompt_layer(xp, lw, tabs, consts)
        xs, rs = _sample_layer(xs, l, lw, tabs, consts, caches, page_table, g_pages)
        rows_p.append(rp)
        rows_s.append(rs)
    outs = [xp, xs]
    for i in range(6):
        outs.append(jnp.stack([r[i] for r in rows_p], axis=0))
        outs.append(jnp.stack([r[i] for r in rows_s], axis=0))
    return tuple(outs)
```

```python
import functools
import math

import numpy as np
import jax
import jax.numpy as jnp
from jax import lax
from jax.experimental import pallas as pl
from jax.experimental.pallas import tpu as pltpu

F32 = jnp.float32
BF16 = jnp.bfloat16

HEAD_DIM = 64
H_FOX = 4
H_NSA = 8
G_NSA = 2
R_NSA = H_NSA // G_NSA
H_STK = 4
CMP_BLOCK = 64
CMP_SHIFT = 6
CMP_HIDDEN = 256
TOP_N = 16
WINDOW = 512
FORCE_BONUS = 1.0e4
ROPE_THETA = 10000.0
EPS = 1e-6
LANES = 128
SCALE = HEAD_DIM ** -0.5

W_FOX = H_FOX * HEAD_DIM
W_NSA = H_NSA * HEAD_DIM
W_GRP = G_NSA * HEAD_DIM
W_STK = H_STK * HEAD_DIM
N_MISC = H_FOX + 3 * H_NSA

C_FQ, C_FKV, C_NQ, C_CMP, C_SEL, C_WIN, C_SQ, C_SKV, C_MISC = 0, 256, 768, 1280, 1536, 1792, 2048, 2304, 2816
W_PROJ = 2944

NSA_HEAD_ORDER = (0, 4, 1, 5, 2, 6, 3, 7)

VMEM_LIMIT = 56 * 1024 * 1024


def _cparams(*sem):
    return pltpu.CompilerParams(dimension_semantics=sem, vmem_limit_bytes=VMEM_LIMIT)


def _iota(shape, dim):
    return lax.broadcasted_iota(jnp.int32, shape, dim)


def _log_sigmoid(z):
    return jnp.minimum(z, 0.0) - jnp.log(1.0 + jnp.exp(-jnp.abs(z)))


def _dot(a, b):
    return jnp.dot(a, b, preferred_element_type=F32)


def _dot_nt(a, b):
    return lax.dot_general(a, b, (((1,), (1,)), ((), ())), preferred_element_type=F32)


def _split3(x):
    x1 = x.astype(BF16)
    r1 = x - x1.astype(F32)
    x2 = r1.astype(BF16)
    x3 = (r1 - x2.astype(F32)).astype(BF16)
    return x1, x2, x3


def _dot_exact_rhs(x, e):
    x1, x2, x3 = _split3(x)
    return _dot(x1, e) + _dot(x2, e) + _dot(x3, e)


def _dot_exact_lhs(e, x):
    x1, x2, x3 = _split3(x)
    return _dot(e, x1) + _dot(e, x2) + _dot(e, x3)


def _head_norm(y, s64, gain):
    w = y.shape[1]
    y2 = (y * y).astype(BF16)
    outs = []
    for c in range(0, w, 256):
        cw = min(256, w - c)
        ms = _dot(y2[:, c:c + cw], s64[:cw, :cw])
        outs.append(y[:, c:c + cw] * lax.rsqrt(ms + EPS))
    yn = outs[0] if len(outs) == 1 else jnp.concatenate(outs, axis=1)
    return yn * gain


def _rope(y, cos, sin):
    w = y.shape[1]
    lane = _iota(y.shape, 1)
    first = (lane & 63) < 32
    sw = jnp.where(first, pltpu.roll(y, w - 32, 1), pltpu.roll(y, 32, 1))
    if w > LANES:
        cos = jnp.concatenate([cos] * (w // LANES), axis=1)
        sin = jnp.concatenate([sin] * (w // LANES), axis=1)
    return y * cos + sw * sin


def _suffix_excl_lanes(x):
    lane = _iota(x.shape, 1)
    y = x
    s = 1
    while s < LANES:
        y = y + jnp.where(lane + s < LANES, pltpu.roll(y, LANES - s, 1), 0.0)
        s *= 2
    return y - x, y[:, 0:1]


def _stack_heads(q, n_heads):
    lane = _iota(q.shape, 1)
    zero = jnp.zeros_like(q)
    return jnp.concatenate([jnp.where((lane >> 6) == h, q, zero) for h in range(n_heads)], axis=0)


def _unstack_heads(o, n_heads, t):
    lane = _iota((t, o.shape[1]), 1)
    out = jnp.zeros((t, o.shape[1]), F32)
    for h in range(n_heads):
        out = out + jnp.where((lane >> 6) == h, o[h * t:(h + 1) * t], 0.0)
    return out


def _stack_nsa(q):
    t = q.shape[0]
    lane = _iota((t, LANES), 1)
    blocks = []
    for c in range(R_NSA):
        qc = q[:, c * LANES:(c + 1) * LANES]
        for g in range(G_NSA):
            blocks.append(jnp.where((lane >> 6) == g, qc, jnp.zeros_like(qc)))
    return jnp.concatenate(blocks, axis=0)


def _unstack_nsa(o, t):
    lane = _iota((t, LANES), 1)
    chunks = []
    for c in range(R_NSA):
        a = o[(2 * c) * t:(2 * c + 1) * t]
        b = o[(2 * c + 1) * t:(2 * c + 2) * t]
        chunks.append(jnp.where((lane >> 6) == 0, a, b))
    return jnp.concatenate(chunks, axis=1)


def _softmax_update(s, mask, m_ref, l_ref, acc_ref, vt):
    if mask is not None:
        s = jnp.where(mask, s, -jnp.inf)
    m_old = m_ref[...]
    m_new = jnp.maximum(m_old, jnp.max(s, axis=1, keepdims=True))
    m_safe = jnp.where(m_new > -jnp.inf, m_new, 0.0)
    alpha = jnp.exp(m_old - m_safe)
    p = jnp.exp(s - m_safe)
    l_ref[...] = alpha * l_ref[...] + jnp.sum(p, axis=1, keepdims=True)
    acc_ref[...] = alpha * acc_ref[...] + _dot_nt(p.astype(BF16), vt)
    m_ref[...] = m_new


def _tile_lanes(x, w):
    return x if w == LANES else jnp.concatenate([x] * (w // LANES), axis=1)


def _strip_softmax(s, mask, r0, strip, m_ref, l_ref, a_ref, p_ref, guard):
    rows = pl.ds(r0, strip)
    if mask is not None:
        s = jnp.where(mask, s, -jnp.inf)
    m_old = m_ref[rows, :]
    m_new = jnp.maximum(m_old, jnp.max(s, axis=1, keepdims=True))
    m_use = jnp.where(m_new > -jnp.inf, m_new, 0.0) if guard else m_new
    alpha = jnp.exp(m_old - m_use)
    p = jnp.exp(s - _tile_lanes(m_use, s.shape[1]))
    l_ref[rows, :] = alpha * l_ref[rows, :] + jnp.sum(p, axis=1, keepdims=True)
    m_ref[rows, :] = m_new
    a_ref[rows, :] = alpha
    p_ref[rows, :] = p.astype(BF16)


def _gate_expand(misc, eg_ref):
    m1, m2, m3 = _split3(misc)
    outs = []
    for j in range(3):
        e = eg_ref[j]
        outs.append(_dot(m1, e) + _dot(m2, e) + _dot(m3, e))
    return outs


def _rope_table_kernel(pos_ref, inv_ref, sgn_ref, cos_ref, sin_ref):
    ang = pos_ref[...] * inv_ref[...]
    cos_ref[...] = jnp.cos(ang)
    sin_ref[...] = jnp.sin(ang) * sgn_ref[...]


def _rope_tables(pos):
    p = pos.shape[0]
    half = HEAD_DIM // 2
    inv = ROPE_THETA ** (-jnp.arange(half, dtype=F32) / half)
    inv128 = jnp.tile(inv, 4)[None, :]
    sgn128 = jnp.tile(jnp.concatenate([-jnp.ones((half,), F32), jnp.ones((half,), F32)]), 2)[None, :]
    return pl.pallas_call(
        _rope_table_kernel,
        out_shape=(jax.ShapeDtypeStruct((p, LANES), F32), jax.ShapeDtypeStruct((p, LANES), F32)),
        name="rope_tables",
    )(pos.astype(F32)[:, None], inv128, sgn128)


def _proj_kernel(x_ref, g_ref, w_ref, bias_ref, gains_ref, cos_ref, sin_ref, s64_ref,
                 qf_ref, fkv_ref, qn_ref, cmp_ref, sel_ref, win_ref, qs_ref, skv_ref, misc_ref):
    x = x_ref[...]
    ms = jnp.mean(x * x, axis=-1, keepdims=True)
    xn = (x * lax.rsqrt(ms + EPS) * g_ref[...]).astype(BF16)
    cos = cos_ref[...]
    sin = sin_ref[...]
    s64 = s64_ref[...]

    def mm(c0, w):
        return _dot_nt(xn, w_ref[c0:c0 + w, :])

    qf_ref[...] = (_head_norm(mm(C_FQ, W_FOX), s64, gains_ref[0:1, :W_FOX]) * SCALE).astype(BF16)
    kv = mm(C_FKV, 2 * W_FOX)
    fkv_ref[:, :W_FOX] = _head_norm(kv[:, :W_FOX], s64, gains_ref[1:2, :W_FOX])
    fkv_ref[:, W_FOX:] = kv[:, W_FOX:]
    qn = _rope(_head_norm(mm(C_NQ, W_NSA), s64, gains_ref[2:3, :W_NSA]), cos, sin)
    qn_ref[...] = (qn * SCALE).astype(BF16)
    cmp_ref[...] = mm(C_CMP, 2 * W_GRP)
    sel = mm(C_SEL, 2 * W_GRP)
    sel_ref[:, :W_GRP] = _rope(_head_norm(sel[:, :W_GRP], s64, gains_ref[3:4, :W_GRP]), cos, sin)
    sel_ref[:, W_GRP:] = sel[:, W_GRP:]
    win = mm(C_WIN, 2 * W_GRP)
    win_ref[:, :W_GRP] = _rope(_head_norm(win[:, :W_GRP], s64, gains_ref[4:5, :W_GRP]), cos, sin)
    win_ref[:, W_GRP:] = win[:, W_GRP:]
    qs_ref[...] = (mm(C_SQ, W_STK) * SCALE).astype(BF16)
    skv_ref[...] = mm(C_SKV, 2 * W_STK)
    c = mm(C_MISC, LANES) + bias_ref[...]
    lane = _iota(c.shape, 1)
    misc_ref[...] = jnp.where(lane < H_FOX, _log_sigmoid(c), 1.0 / (1.0 + jnp.exp(-c)))


def _proj(x2d, lw, cos, sin, n_pos_blocks, tn):
    n, d = x2d.shape
    nt = n // tn
    row = lambda i: (i, 0)
    const = lambda i: (0, 0)
    widths = (W_FOX, 2 * W_FOX, W_NSA, 2 * W_GRP, 2 * W_GRP, 2 * W_GRP, W_STK, 2 * W_STK, LANES)
    dtypes = (BF16, F32, BF16, F32, F32, F32, BF16, F32, F32)
    out_shape = tuple(jax.ShapeDtypeStruct((n, w), dt) for w, dt in zip(widths, dtypes))
    out_specs = tuple(pl.BlockSpec((tn, w), row) for w in widths)
    pos_map = lambda i: (i % n_pos_blocks, 0)
    return pl.pallas_call(
        _proj_kernel,
        grid=(nt,),
        in_specs=[
            pl.BlockSpec((tn, d), row),
            pl.BlockSpec((1, d), const),
            pl.BlockSpec((W_PROJ, d), const),
            pl.BlockSpec((1, LANES), const),
            pl.BlockSpec((8, W_NSA), const),
            pl.BlockSpec((tn, LANES), pos_map),
            pl.BlockSpec((tn, LANES), pos_map),
            pl.BlockSpec((256, 256), const),
        ],
        out_specs=out_specs,
        out_shape=out_shape,
        compiler_params=_cparams("arbitrary"),
        name="proj",
    )(x2d, lw["g_attn"], lw["w_in"], lw["bias_misc"], lw["gains"], cos, sin, lw["s64"])


def _cumsum_kernel(misc_ref, ccol_ref, crow_ref, *, blk):
    t = misc_ref.shape[1]
    r = _iota((blk, blk), 0)
    c = _iota((blk, blk), 1)
    tri = jnp.where(r >= c, 1.0, 0.0).astype(BF16)
    carry = jnp.zeros((1, LANES), F32)
    for i in range(t // blk):
        x = misc_ref[0, i * blk:(i + 1) * blk, :]
        cs = _dot_exact_lhs(tri, x) + carry
        carry = cs[blk - 1:blk, :]
        ccol_ref[0, i * blk:(i + 1) * blk, :] = cs
        crow_ref[0, i] = cs.T[0:8, :]


def _cumsum(misc3, blk):
    b, t, _ = misc3.shape
    return pl.pallas_call(
        functools.partial(_cumsum_kernel, blk=blk),
        grid=(b,),
        in_specs=[pl.BlockSpec((1, t, LANES), lambda i: (i, 0, 0))],
        out_specs=(pl.BlockSpec((1, t, LANES), lambda i: (i, 0, 0)),
                   pl.BlockSpec((1, t // blk, 8, blk), lambda i: (i, 0, 0, 0))),
        out_shape=(jax.ShapeDtypeStruct((b, t, LANES), F32),
                   jax.ShapeDtypeStruct((b, t // blk, 8, blk), F32)),
        compiler_params=_cparams("arbitrary"),
        name="logf_cumsum",
    )(misc3)


def _fox_prompt_kernel(q_ref, kv_ref, ccol_ref, crow_ref, o_ref, kb, vb, q4_s, cq_s, s_s, p_s, m_s, l_s, a_s, acc_s,
                       *, tq, strip):
    qi = pl.program_id(1)

    @pl.when(qi == 0)
    def _():
        kb[...] = kv_ref[0, :, :W_FOX].astype(BF16)
        vb[...] = kv_ref[0, :, W_FOX:].astype(BF16)

    q4_s[...] = _stack_heads(q_ref[0], H_FOX)
    cc = ccol_ref[0]
    cq_s[...] = jnp.concatenate([jnp.broadcast_to(cc[:, h:h + 1], (tq, LANES)) for h in range(H_FOX)], axis=0)
    m_s[...] = jnp.full(m_s.shape, -jnp.inf, F32)
    l_s[...] = jnp.zeros(l_s.shape, F32)
    acc_s[...] = jnp.zeros(acc_s.shape, F32)
    q0 = qi * tq

    def tile(kj, diag):
        k0 = pl.multiple_of(kj * tq, tq)
        s_s[...] = _dot_nt(q4_s[...], kb[pl.ds(k0, tq), :])
        cr = crow_ref[0, kj]
        kpos = k0 + _iota((1, tq), 1)
        for h in range(H_FOX):
            ck = cr[h:h + 1, :]

            def strip_body(i, c, h=h, ck=ck):
                r0 = h * tq + i * strip
                s = s_s[pl.ds(r0, strip), :] + _tile_lanes(cq_s[pl.ds(r0, strip), :], tq) - ck
                mask = (kpos <= q0 + i * strip + _iota((strip, 1), 0)) if diag else None
                _strip_softmax(s, mask, r0, strip, m_s, l_s, a_s, p_s, False)
                return c

            for i in range(tq // strip):
                strip_body(i, 0)
        acc_s[...] = _tile_lanes(a_s[...], W_FOX) * acc_s[...] + _dot(p_s[...], vb[pl.ds(k0, tq), :])

    def body(kj, c):
        tile(kj, False)
        return c

    lax.fori_loop(0, qi, body, 0)
    tile(qi, True)
    o4 = acc_s[...] / _tile_lanes(jnp.maximum(l_s[...], 1e-30), W_FOX)
    o_ref[0] = _unstack_heads(o4, H_FOX, tq)


def _fox_prompt(q3, kv3, ccol, crow, tq, tk):
    b, t, _ = q3.shape
    assert tq == tk
    rows = H_FOX * tq
    return pl.pallas_call(
        functools.partial(_fox_prompt_kernel, tq=tq, strip=32),
        grid=(b, t // tq),
        in_specs=[
            pl.BlockSpec((1, tq, W_FOX), lambda i, j: (i, j, 0)),
            pl.BlockSpec((1, t, 2 * W_FOX), lambda i, j: (i, 0, 0)),
            pl.BlockSpec((1, tq, LANES), lambda i, j: (i, j, 0)),
            pl.BlockSpec((1, t // tk, 8, tk), lambda i, j: (i, 0, 0, 0)),
        ],
        out_specs=pl.BlockSpec((1, tq, W_FOX), lambda i, j: (i, j, 0)),
        out_shape=jax.ShapeDtypeStruct((b, t, W_FOX), F32),
        scratch_shapes=[
            pltpu.VMEM((t, W_FOX), BF16), pltpu.VMEM((t, W_FOX), BF16),
            pltpu.VMEM((rows, W_FOX), BF16), pltpu.VMEM((rows, LANES), F32),
            pltpu.VMEM((rows, tq), F32), pltpu.VMEM((rows, tq), BF16),
            pltpu.VMEM((rows, LANES), F32), pltpu.VMEM((rows, LANES), F32), pltpu.VMEM((rows, LANES), F32),
            pltpu.VMEM((rows, W_FOX), F32),
        ],
        compiler_params=_cparams("arbitrary", "arbitrary"),
        name="fox_prompt",
    )(q3, kv3, ccol, crow)


def _stick_prompt_kernel(q_ref, kv_ref, o_ref, kb, vb, tri_s, q4_s, z_s, hi_s, lo_s, rc_s, p_s, rs_s, r_s, acc_s,
                         *, tq, strip):
    qi = pl.program_id(1)

    @pl.when(qi == 0)
    def _():
        kb[...] = kv_ref[0, :, :W_STK].astype(BF16)
        vb[...] = kv_ref[0, :, W_STK:].astype(BF16)
        tri_s[...] = jnp.where(_iota((tq, tq), 0) > _iota((tq, tq), 1), 1.0, 0.0).astype(BF16)

    q4_s[...] = _stack_heads(q_ref[0], H_STK)
    r_s[...] = jnp.zeros(r_s.shape, F32)
    acc_s[...] = jnp.zeros(acc_s.shape, F32)
    q0 = qi * tq
    n_strips = H_STK * tq // strip

    def tile(kj, diag):
        k0 = pl.multiple_of(kj * tq, tq)
        z_s[...] = _dot_nt(q4_s[...], kb[pl.ds(k0, tq), :])
        kpos = k0 + _iota((1, tq), 1)

        def strip_mask(i):
            tq_off = (i * strip) & (tq - 1)
            return kpos < q0 + tq_off + _iota((strip, 1), 0)

        def pass1(i, c):
            rows = pl.ds(i * strip, strip)
            z = z_s[rows, :]
            ls = _log_sigmoid(z)
            lk = ls - z
            if diag:
                lk = jnp.where(strip_mask(i), lk, 0.0)
            hi = lk.astype(BF16)
            hi_s[rows, :] = hi
            lo_s[rows, :] = (lk - hi.astype(F32)).astype(BF16)
            z_s[rows, :] = ls
            rs_s[rows, :] = jnp.sum(lk, axis=1, keepdims=True)
            return c

        for i in range(n_strips):
            pass1(i, 0)
        tri = tri_s[...]
        rc_s[...] = _dot(hi_s[...], tri) + _dot(lo_s[...], tri)

        def pass2(i, c):
            rows = pl.ds(i * strip, strip)
            a = jnp.exp(z_s[rows, :] + rc_s[rows, :] + r_s[rows, :])
            if diag:
                a = jnp.where(strip_mask(i), a, 0.0)
            p_s[rows, :] = a.astype(BF16)
            return c

        for i in range(n_strips):
            pass2(i, 0)
        acc_s[...] = acc_s[...] + _dot(p_s[...], vb[pl.ds(k0, tq), :])
        r_s[...] = r_s[...] + rs_s[...]

    tile(qi, True)

    def body(it, c):
        tile(qi - 1 - it, False)
        return c

    lax.fori_loop(0, qi, body, 0)
    o_ref[0] = _unstack_heads(acc_s[...], H_STK, tq)


def _stick_prompt(q3, kv3, tq, tk):
    b, t, _ = q3.shape
    assert tq == tk and (tq & (tq - 1)) == 0
    rows = H_STK * tq
    return pl.pallas_call(
        functools.partial(_stick_prompt_kernel, tq=tq, strip=32),
        grid=(b, t // tq),
        in_specs=[
            pl.BlockSpec((1, tq, W_STK), lambda i, j: (i, j, 0)),
            pl.BlockSpec((1, t, 2 * W_STK), lambda i, j: (i, 0, 0)),
        ],
        out_specs=pl.BlockSpec((1, tq, W_STK), lambda i, j: (i, j, 0)),
        out_shape=jax.ShapeDtypeStruct((b, t, W_STK), F32),
        scratch_shapes=[
            pltpu.VMEM((t, W_STK), BF16), pltpu.VMEM((t, W_STK), BF16), pltpu.VMEM((tq, tq), BF16),
            pltpu.VMEM((rows, W_STK), BF16),
            pltpu.VMEM((rows, tq), F32), pltpu.VMEM((rows, tq), BF16), pltpu.VMEM((rows, tq), BF16),
            pltpu.VMEM((rows, tq), F32), pltpu.VMEM((rows, tq), BF16),
            pltpu.VMEM((rows, 1), F32), pltpu.VMEM((rows, 1), F32),
            pltpu.VMEM((rows, W_STK), F32),
        ],
        compiler_params=_cparams("arbitrary", "arbitrary"),
        name="stick_prompt",
    )(q3, kv3)


def _compress_body(x_refs, pe_ref, w1k_ref, w1v_ref, w2k_ref, w2v_ref, gkc_ref, cos_ref, sin_ref, s64_ref,
                   kc_ref, vc_ref, xs):
    off = 0
    for r in x_refs:
        rows = r.shape[-2]
        v = r[...].reshape(rows, 2 * W_GRP)
        xs[0, off:off + rows, :] = v[:, :LANES]
        xs[1, off:off + rows, :] = v[:, LANES:]
        off += rows
    nblk = off // CMP_BLOCK
    lane = _iota((nblk, LANES), 1)
    lo_half = lane < HEAD_DIM
    hk = jnp.zeros((2 * nblk, CMP_HIDDEN), F32)
    hv = jnp.zeros((2 * nblk, CMP_HIDDEN), F32)
    for i4 in range(CMP_BLOCK // 4):
        x = [[xs[h, pl.ds(4 * i4 + j, nblk, stride=CMP_BLOCK), :] for h in range(2)] for j in range(4)]
        sw = [[pltpu.roll(v, HEAD_DIM, 1) for v in xj] for xj in x]

        def chunk_at(j, c, pos):
            return x[j][c // 2] if (c % 2) == pos else sw[j][c // 2]

        packed = []
        for c in range(4):
            left = jnp.where(lo_half, chunk_at(0, c, 0), chunk_at(1, c, 1))
            right = jnp.where(lo_half, chunk_at(2, c, 0), chunk_at(3, c, 1))
            packed.append(jnp.concatenate([left, right], axis=1))
        ak = jnp.concatenate([packed[0], packed[1]], axis=0) + pe_ref[0, i4:i4 + 1, :]
        av = jnp.concatenate([packed[2], packed[3]], axis=0) + pe_ref[1, i4:i4 + 1, :]
        hk = hk + _dot(ak.astype(BF16), w1k_ref[256 * i4:256 * (i4 + 1), :])
        hv = hv + _dot(av.astype(BF16), w1v_ref[256 * i4:256 * (i4 + 1), :])
    gk = jax.nn.gelu(hk).astype(BF16)
    gv = jax.nn.gelu(hv).astype(BF16)
    kc = _dot(gk[:nblk], w2k_ref[0]) + _dot(gk[nblk:], w2k_ref[1])
    kc = _rope(_head_norm(kc, s64_ref[...], gkc_ref[...]), cos_ref[...], sin_ref[...])
    kc_ref[...] = kc.reshape(kc_ref.shape)
    vc = _dot(gv[:nblk], w2v_ref[0]) + _dot(gv[nblk:], w2v_ref[1])
    vc_ref[...] = vc.reshape(vc_ref.shape)


def _compress_kernel(*refs, n_x, n_prefetch):
    refs = refs[n_prefetch:]
    _compress_body(refs[:n_x], *refs[n_x:])


def _compress_weight_specs(nmap):
    return [
        pl.BlockSpec((2, CMP_BLOCK // 4, 256), nmap(3)),
        pl.BlockSpec((CMP_BLOCK * HEAD_DIM, CMP_HIDDEN), nmap(2)),
        pl.BlockSpec((CMP_BLOCK * HEAD_DIM, CMP_HIDDEN), nmap(2)),
        pl.BlockSpec((2, CMP_HIDDEN, LANES), nmap(3)),
        pl.BlockSpec((2, CMP_HIDDEN, LANES), nmap(3)),
        pl.BlockSpec((1, LANES), nmap(2)),
    ]


def _compress_contig(x3, lw, cos, sin):
    b, t, _ = x3.shape
    nblk = t // CMP_BLOCK
    nmap = lambda n: (lambda i: (0,) * n)
    return pl.pallas_call(
        functools.partial(_compress_kernel, n_x=1, n_prefetch=0),
        grid=(b,),
        in_specs=[pl.BlockSpec((1, t, 2 * W_GRP), lambda i: (i, 0, 0))] + _compress_weight_specs(nmap) + [
            pl.BlockSpec((nblk, LANES), nmap(2)),
            pl.BlockSpec((nblk, LANES), nmap(2)),
            pl.BlockSpec((256, 256), nmap(2)),
        ],
        out_specs=(pl.BlockSpec((1, nblk, LANES), lambda i: (i, 0, 0)),
                   pl.BlockSpec((1, nblk, LANES), lambda i: (i, 0, 0))),
        out_shape=(jax.ShapeDtypeStruct((b, nblk, LANES), F32), jax.ShapeDtypeStruct((b, nblk, LANES), F32)),
        scratch_shapes=[pltpu.VMEM((2, t, LANES), F32)],
        compiler_params=_cparams("arbitrary"),
        name="nsa_compress",
    )(x3, lw["pe4"], lw["w1k"], lw["w1v"], lw["w2k"], lw["w2v"], lw["gkc"], cos, sin, lw["s64"])


def _block_select(score, visible, lane_n, nb, n_sel, group_lanes):
    cnt = jnp.zeros(score.shape, F32)
    n_slots = score.shape[1] // group_lanes
    lane = _iota(score.shape, 1)
    for j in range(nb):
        if n_slots == 1:
            vj = score[:, j:j + 1]
        else:
            vj = score[:, j:j + 1]
            for sl in range(1, n_slots):
                vj = jnp.where(lane >= sl * group_lanes, score[:, sl * group_lanes + j:sl * group_lanes + j + 1], vj)
        ahead = (vj > score) | ((vj == score) & (lane_n > j))
        cnt = cnt + jnp.where(ahead, 1.0, 0.0)
    return visible & (cnt < n_sel)


def _block_select_t(score, sc_t, nb, n_sel):
    r = score.shape[0]
    for c in range(r // LANES):
        sc_t[:, c * LANES:(c + 1) * LANES] = score[c * LANES:(c + 1) * LANES, :].T
    n_row = _iota((nb, r), 0)
    outs = []
    for g in range(LANES // HEAD_DIM):
        base = g * HEAD_DIM
        blk = sc_t[base:base + nb, :]
        cnt = jnp.zeros((nb, r), F32)
        for j in range(nb):
            vj = sc_t[base + j:base + j + 1, :]
            ahead = (vj > blk) | ((vj == blk) & (n_row > j))
            cnt = cnt + jnp.where(ahead, 1.0, 0.0)
        sel = jnp.where(cnt < n_sel, 1.0, 0.0)
        outs.append(jnp.concatenate([sel, jnp.zeros((HEAD_DIM - nb, r), F32)], axis=0) if nb < HEAD_DIM else sel)
    sel_t = jnp.concatenate(outs, axis=0)
    return jnp.concatenate([sel_t[:, c * LANES:(c + 1) * LANES].T for c in range(r // LANES)], axis=0)


def _masked_softmax_rows(s, mask):
    s = jnp.where(mask, s, -jnp.inf)
    m = jnp.max(s, axis=1, keepdims=True)
    m = jnp.where(m > -jnp.inf, m, 0.0)
    e = jnp.where(mask, jnp.exp(s - m), 0.0)
    return e / jnp.maximum(jnp.sum(e, axis=1, keepdims=True), 1e-30)


def _nsa_prompt_kernel(q_ref, kc_ref, vc_ref, sel_ref, win_ref, misc_ref, eg_ref, eb_ref, o_ref,
                       selb, winb, kcs, vcs, qs_s, selm_s, sct_s, me_s, s_s, p_s, m_s, l_s, a_s, acc_s, sw_s, pw_s,
                       *, tq, tk, nb, span, strip):
    qi = pl.program_id(1)

    @pl.when(qi == 0)
    def _():
        selb[...] = sel_ref[0].astype(BF16)
        winb[...] = win_ref[0].astype(BF16)
        lane = _iota((nb, LANES), 1)
        kc = kc_ref[0]
        vc = vc_ref[0]
        kcs[...] = jnp.zeros(kcs.shape, BF16)
        vcs[...] = jnp.zeros(vcs.shape, BF16)
        for g in range(G_NSA):
            keep = (lane >> 6) == g
            kcs[g * HEAD_DIM:g * HEAD_DIM + nb, :] = jnp.where(keep, kc, 0.0).astype(BF16)
            vcs[g * HEAD_DIM:g * HEAD_DIM + nb, :] = jnp.where(keep, vc, 0.0).astype(BF16)

    q = q_ref[0]
    q0 = qi * tq
    tpos = q0 + _iota((tq, 1), 0)
    lane = _iota((tq, LANES), 1)
    n_idx = lane & 63
    half = lane >> 6

    cmask = ((n_idx << CMP_SHIFT) + (CMP_BLOCK - 1) <= tpos) & (n_idx < nb)
    imp = jnp.zeros((tq, LANES), F32)
    oc_chunks = []
    for c in range(R_NSA):
        s = _dot_nt(q[:, c * LANES:(c + 1) * LANES], kcs[...])
        s = jnp.where(cmask, s, -jnp.inf)
        m0 = jnp.max(jnp.where(half == 0, s, -jnp.inf), axis=1, keepdims=True)
        m1 = jnp.max(jnp.where(half == 1, s, -jnp.inf), axis=1, keepdims=True)
        m = jnp.where(half == 0, m0, m1)
        m = jnp.where(m > -jnp.inf, m, 0.0)
        e = jnp.where(cmask, jnp.exp(s - m), 0.0)
        d0 = jnp.sum(jnp.where(half == 0, e, 0.0), axis=1, keepdims=True)
        d1 = jnp.sum(jnp.where(half == 1, e, 0.0), axis=1, keepdims=True)
        pc = e / jnp.maximum(jnp.where(half == 0, d0, d1), 1e-30)
        imp = imp + pc
        oc_chunks.append(_dot(pc.astype(BF16), vcs[...]))
    o_c = jnp.concatenate(oc_chunks, axis=1)

    visible = ((n_idx << CMP_SHIFT) <= tpos) & (n_idx < nb)
    cur = tpos >> CMP_SHIFT
    forced = (n_idx == 0) | (n_idx == cur) | (n_idx == cur - 1)
    score = jnp.where(visible, imp + FORCE_BONUS * jnp.where(forced, 1.0, 0.0), -jnp.inf)
    picked = _block_select_t(score, sct_s, nb, min(TOP_N, nb))
    selm_s[...] = jnp.where(visible, picked, 0.0).astype(BF16)

    qs_s[...] = _stack_nsa(q)
    n_blocks = G_NSA * R_NSA

    m_s[...] = jnp.full(m_s.shape, -jnp.inf, F32)
    l_s[...] = jnp.zeros(l_s.shape, F32)
    acc_s[...] = jnp.zeros(acc_s.shape, F32)

    def sel_tile(kj, diag):
        k0 = pl.multiple_of(kj * tk, tk)
        s_s[...] = _dot_nt(qs_s[...], selb[pl.ds(k0, tk), :W_GRP])
        for g in range(G_NSA):
            me_s[g] = _dot(selm_s[...], eb_ref[g, kj])
        kpos = k0 + _iota((1, tk), 1)
        for blk in range(n_blocks):

            def strip_body(i, c, blk=blk):
                off = i * strip
                r0 = blk * tq + i * strip
                mk = me_s[blk % G_NSA, pl.ds(off, strip), :] > 0.5
                if diag:
                    mk = mk & (kpos <= q0 + i * strip + _iota((strip, 1), 0))
                _strip_softmax(s_s[pl.ds(r0, strip), :], mk, r0, strip, m_s, l_s, a_s, p_s, True)
                return c

            for i in range(tq // strip):
                strip_body(i, 0)
        acc_s[...] = a_s[...] * acc_s[...] + _dot(p_s[...], selb[pl.ds(k0, tk), W_GRP:])

    def body(kj, c):
        sel_tile(kj, False)
        return c

    nk = (q0 + tq - 1) // tk + 1
    lax.fori_loop(0, nk - 1, body, 0)
    sel_tile(nk - 1, True)
    o_s = _unstack_nsa(acc_s[...] / jnp.maximum(l_s[...], 1e-30), tq)

    start = pl.multiple_of(jnp.maximum(q0 + tq - span, 0), LANES)
    sw_s[...] = _dot_nt(qs_s[...], winb[pl.ds(start, span), :W_GRP])
    kposw = start + _iota((1, span), 1)

    def wstrip(i, c):
        rows = pl.ds(i * strip, strip)
        tp = q0 + ((i * strip) & (tq - 1)) + _iota((strip, 1), 0)
        sw = jnp.where((kposw <= tp) & (kposw > tp - WINDOW), sw_s[rows, :], -jnp.inf)
        e = jnp.exp(sw - jnp.max(sw, axis=1, keepdims=True))
        pw = e / jnp.maximum(jnp.sum(e, axis=1, keepdims=True), 1e-30)
        pw_s[rows, :] = pw.astype(BF16)
        return c

    for i in range(n_blocks * tq // strip):
        wstrip(i, 0)
    o_w = _unstack_nsa(_dot(pw_s[...], winb[pl.ds(start, span), W_GRP:]), tq)

    g_c, g_s, g_w = _gate_expand(misc_ref[0], eg_ref)
    o_ref[0] = g_c * o_c + g_s * o_s + g_w * o_w


def _nsa_prompt(q3, kc, vc, sel3, win3, misc3, consts, tq, tk):
    b, t, _ = q3.shape
    nb = t // CMP_BLOCK
    span = min(WINDOW + tq, t)
    rows = H_NSA * tq
    assert (tq & (tq - 1)) == 0
    return pl.pallas_call(
        functools.partial(_nsa_prompt_kernel, tq=tq, tk=tk, nb=nb, span=span, strip=32),
        grid=(b, t // tq),
        in_specs=[
            pl.BlockSpec((1, tq, W_NSA), lambda i, j: (i, j, 0)),
            pl.BlockSpec((1, nb, LANES), lambda i, j: (i, 0, 0)),
            pl.BlockSpec((1, nb, LANES), lambda i, j: (i, 0, 0)),
            pl.BlockSpec((1, t, 2 * W_GRP), lambda i, j: (i, 0, 0)),
            pl.BlockSpec((1, t, 2 * W_GRP), lambda i, j: (i, 0, 0)),
            pl.BlockSpec((1, tq, LANES), lambda i, j: (i, j, 0)),
            pl.BlockSpec((3, LANES, W_NSA), lambda i, j: (0, 0, 0)),
            pl.BlockSpec((G_NSA, t // tk, LANES, tk), lambda i, j: (0, 0, 0, 0)),
        ],
        out_specs=pl.BlockSpec((1, tq, W_NSA), lambda i, j: (i, j, 0)),
        out_shape=jax.ShapeDtypeStruct((b, t, W_NSA), F32),
        scratch_shapes=[
            pltpu.VMEM((t, 2 * W_GRP), BF16), pltpu.VMEM((t, 2 * W_GRP), BF16),
            pltpu.VMEM((LANES, LANES), BF16), pltpu.VMEM((LANES, LANES), BF16),
            pltpu.VMEM((rows, LANES), BF16), pltpu.VMEM((tq, LANES), BF16), pltpu.VMEM((LANES, tq), F32),
            pltpu.VMEM((G_NSA, tq, tk), F32),
            pltpu.VMEM((rows, tk), F32), pltpu.VMEM((rows, tk), BF16),
            pltpu.VMEM((rows, LANES), F32), pltpu.VMEM((rows, LANES), F32), pltpu.VMEM((rows, LANES), F32),
            pltpu.VMEM((rows, LANES), F32),
            pltpu.VMEM((rows, span), F32), pltpu.VMEM((rows, span), BF16),
        ],
        compiler_params=_cparams("arbitrary", "arbitrary"),
        name="nsa_prompt",
    )(q3, kc, vc, sel3, win3, misc3, consts["egate"], consts["eb_prompt"])


def _compress_paged_kernel(pt_ref, *refs, g_pages):
    x_refs = refs[:g_pages]
    (pe_ref, w1k_ref, w1v_ref, w2k_ref, w2v_ref, gkc_ref, cos_ref, sin_ref, s64_ref, kc_ref, vc_ref) = refs[g_pages:]
    n_sl = G_NSA * g_pages
    lo = _iota((n_sl, LANES), 1) < CMP_BLOCK
    hs = []
    for kv, w1 in enumerate((w1k_ref, w1v_ref)):
        h = jnp.zeros((2 * n_sl, CMP_HIDDEN), F32)
        for o in range(HEAD_DIM // 8):
            rows = [r[kv, g * HEAD_DIM + 8 * o:g * HEAD_DIM + 8 * o + 8, :] for g in range(G_NSA) for r in x_refs]
            by_d = jnp.swapaxes(jnp.stack(rows), 0, 1)
            for q4 in (2 * o, 2 * o + 1):
                x = [by_d[4 * (q4 - 2 * o) + j] for j in range(4)]
                sw = [pltpu.roll(v, CMP_BLOCK, 1) for v in x]
                a0 = jnp.concatenate([jnp.where(lo, x[0], sw[1]), jnp.where(lo, x[2], sw[3])], axis=1)
                a1 = jnp.concatenate([jnp.where(lo, sw[0], x[1]), jnp.where(lo, sw[2], x[3])], axis=1)
                a = jnp.concatenate([a0, a1], axis=0) + pe_ref[kv, q4:q4 + 1, :]
                h = h + _dot(a.astype(BF16), w1[256 * q4:256 * (q4 + 1), :])
        hs.append(h)
    gk = jax.nn.gelu(hs[0]).astype(BF16)
    gv = jax.nn.gelu(hs[1]).astype(BF16)

    def second(gx, w2_ref):
        outs = []
        for par in range(2):
            r0 = par * n_sl
            outs.append(_dot(gx[r0:r0 + g_pages], w2_ref[0]) + _dot(gx[r0 + g_pages:r0 + n_sl], w2_ref[1]))
        return jnp.concatenate(outs, axis=0)

    kc = _rope(_head_norm(second(gk, w2k_ref), s64_ref[...], gkc_ref[...]), cos_ref[...], sin_ref[...])
    vc = second(gv, w2v_ref)
    for par in range(2):
        kc_ref[0, 0, par] = kc[par * g_pages:(par + 1) * g_pages]
        vc_ref[0, 0, par] = vc[par * g_pages:(par + 1) * g_pages]


def _compress_paged(cache5, layer, page_table, lw, cos, sin, g_pages):
    db, n_pages = page_table.shape
    page = cache5.shape[4]
    ns = n_pages // g_pages
    nmap = lambda n: (lambda b, s, pt: (0,) * n)

    def page_spec(i):
        return pl.BlockSpec((None, None, 2, W_GRP, page),
                            lambda b, s, pt, i=i: (layer, pt[b, s * g_pages + i], 0, 0, 0))

    out_spec = pl.BlockSpec((1, 1, 2, g_pages, LANES), lambda b, s, pt: (b, s, 0, 0, 0))
    grid_spec = pltpu.PrefetchScalarGridSpec(
        num_scalar_prefetch=1,
        grid=(db, ns),
        in_specs=[page_spec(i) for i in range(g_pages)] + _compress_weight_specs(nmap) + [
            pl.BlockSpec((2 * g_pages, LANES), lambda b, s, pt: (s, 0)),
            pl.BlockSpec((2 * g_pages, LANES), lambda b, s, pt: (s, 0)),
            pl.BlockSpec((256, 256), nmap(2)),
        ],
        out_specs=(out_spec, out_spec),
    )
    shp = jax.ShapeDtypeStruct((db, ns, 2, g_pages, LANES), F32)
    kc, vc = pl.pallas_call(
        functools.partial(_compress_paged_kernel, g_pages=g_pages),
        grid_spec=grid_spec,
        out_shape=(shp, shp),
        compiler_params=_cparams("arbitrary", "arbitrary"),
        name="nsa_compress_paged",
    )(page_table, *([cache5] * g_pages), lw["pe4t"], lw["w1kt"], lw["w1vt"], lw["w2k"], lw["w2v"], lw["gkc"],
      cos, sin, lw["s64"])
    nb_past = n_pages * page // CMP_BLOCK
    order = lambda a: a.transpose(0, 1, 3, 2, 4).reshape(db, nb_past, LANES)
    return order(kc), order(vc)


def _fox_decode_kernel(pt_ref, *refs, g_pages, n_pages):
    kv_refs = refs[:g_pages]
    lf_ref, q_ref, knew_ref, lfn_ref, o_ref, m_s, l_s, acc_s, carry_s, nc_s = refs[g_pages:]
    step = pl.program_id(1)
    bi = pl.program_id(0)
    tqn = q_ref.shape[1]
    rows = H_FOX * tqn
    q4 = _stack_heads(q_ref[0], H_FOX)
    tqv = _iota((rows, 1), 0) & (tqn - 1)

    def expand_heads(x):
        return jnp.concatenate([jnp.broadcast_to(x[h:h + 1, :], (tqn, x.shape[1])) for h in range(H_FOX)], axis=0)

    @pl.when(step == 0)
    def _():
        lfn = lfn_ref[0]
        lane8 = _iota(lfn.shape, 1)
        cum = lfn
        s = 1
        while s < tqn:
            cum = cum + jnp.where(lane8 >= s, pltpu.roll(cum, s, 1), 0.0)
            s *= 2
        e = expand_heads(cum)
        lane = _iota(e.shape, 1)
        nc = jnp.sum(jnp.where(lane == tqv, e, 0.0), axis=1, keepdims=True)
        nc_s[...] = nc
        carry_s[...] = jnp.zeros(carry_s.shape, F32)
        m_s[...] = jnp.full(m_s.shape, -jnp.inf, F32)
        l_s[...] = jnp.zeros(l_s.shape, F32)
        acc_s[...] = jnp.zeros(acc_s.shape, F32)
        sc = _dot(q4, knew_ref[0, 0].astype(BF16)) + nc - e
        _softmax_update(sc, lane <= tqv, m_s, l_s, acc_s, knew_ref[0, 1].astype(BF16))

    kt = jnp.concatenate([r[0].astype(BF16) for r in kv_refs], axis=1)
    vt = jnp.concatenate([r[1].astype(BF16) for r in kv_refs], axis=1)
    carry = carry_s[...]
    xs = []
    for i in range(g_pages):
        pg = pt_ref[bi, n_pages - 1 - (step * g_pages + i)]
        pair = lf_ref[pg >> 1]
        lf = jnp.where((pg & 1) == 1, pair[H_FOX:], pair[:H_FOX])
        excl, tot = _suffix_excl_lanes(lf)
        xs.append(excl + carry)
        carry = carry + tot
    carry_s[...] = carry
    bias = expand_heads(jnp.concatenate(xs, axis=1))
    _softmax_update(_dot(q4, kt) + nc_s[...] + bias, None, m_s, l_s, acc_s, vt)

    @pl.when(step == pl.num_programs(1) - 1)
    def _():
        o_ref[0] = _unstack_heads(acc_s[...] / jnp.maximum(l_s[...], 1e-30), H_FOX, tqn)


def _fox_decode(q3, cache5, logf4, layer, page_table, knew, lfn, g_pages):
    db, n_pages = page_table.shape
    page = cache5.shape[4]
    tqn = q3.shape[1]
    ns = n_pages // g_pages
    rows = H_FOX * tqn

    def pg(b, s, pt, i):
        return pt[b, n_pages - 1 - (s * g_pages + i)]

    kv_specs = [pl.BlockSpec((None, None, 2, W_FOX, page), lambda b, s, pt, i=i: (layer, pg(b, s, pt, i), 0, 0, 0))
                for i in range(g_pages)]
    n_pool = logf4.shape[1]
    lf_pairs = logf4.reshape(logf4.shape[0], n_pool // 2, 2 * H_FOX, page)
    lf_spec = pl.BlockSpec((None, n_pool // 2, 2 * H_FOX, page), lambda b, s, pt: (layer, 0, 0, 0))
    grid_spec = pltpu.PrefetchScalarGridSpec(
        num_scalar_prefetch=1,
        grid=(db, ns),
        in_specs=kv_specs + [lf_spec] + [
            pl.BlockSpec((1, tqn, W_FOX), lambda b, s, pt: (b, 0, 0)),
            pl.BlockSpec((1, 2, W_FOX, page), lambda b, s, pt: (b, 0, 0, 0)),
            pl.BlockSpec((1, 8, LANES), lambda b, s, pt: (b, 0, 0)),
        ],
        out_specs=pl.BlockSpec((1, tqn, W_FOX), lambda b, s, pt: (b, 0, 0)),
        scratch_shapes=[
            pltpu.VMEM((rows, 1), F32), pltpu.VMEM((rows, 1), F32), pltpu.VMEM((rows, W_FOX), F32),
            pltpu.VMEM((H_FOX, 1), F32), pltpu.VMEM((rows, 1), F32),
        ],
    )
    return pl.pallas_call(
        functools.partial(_fox_decode_kernel, g_pages=g_pages, n_pages=n_pages),
        grid_spec=grid_spec,
        out_shape=jax.ShapeDtypeStruct((db, tqn, W_FOX), F32),
        compiler_params=_cparams("arbitrary", "arbitrary"),
        name="fox_decode",
    )(page_table, *([cache5] * g_pages), lf_pairs, q3, knew, lfn)


def _stick_decode_kernel(pt_ref, *refs, g_pages):
    kv_refs = refs[:g_pages]
    q_ref, knew_ref, o_ref, r_s, acc_s = refs[g_pages:]
    step = pl.program_id(1)
    tqn = q_ref.shape[1]
    rows = H_STK * tqn
    q4 = _stack_heads(q_ref[0], H_STK)
    tqv = _iota((rows, 1), 0) & (tqn - 1)

    tri = jnp.where(_iota((LANES, LANES), 0) > _iota((LANES, LANES), 1), 1.0, 0.0).astype(BF16)

    def sweep(z, mask, r_col):
        ls = _log_sigmoid(z)
        lk = ls - z
        if mask is not None:
            lk = jnp.where(mask, lk, 0.0)
        n_ch = z.shape[1] // LANES
        chunks = [lk[:, i * LANES:(i + 1) * LANES] for i in range(n_ch)]
        st = chunks[0] if n_ch == 1 else jnp.concatenate(chunks, axis=0)
        hi = st.astype(BF16)
        lo = (st - hi.astype(F32)).astype(BF16)
        ex = _dot(hi, tri) + _dot(lo, tri)
        rcs = []
        for i in range(n_ch):
            rcs.append(ex[i * rows:(i + 1) * rows] + r_col)
            r_col = r_col + jnp.sum(chunks[i], axis=1, keepdims=True)
        rc = rcs[0] if n_ch == 1 else jnp.concatenate(rcs, axis=1)
        a = jnp.exp(ls + rc)
        if mask is not None:
            a = jnp.where(mask, a, 0.0)
        return a, r_col

    @pl.when(step == 0)
    def _():
        z = _dot(q4, knew_ref[0, 0].astype(BF16))
        lane = _iota(z.shape, 1)
        a, r_col = sweep(z, lane < tqv, jnp.zeros((rows, 1), F32))
        r_s[...] = r_col
        acc_s[...] = _dot_nt(a.astype(BF16), knew_ref[0, 1].astype(BF16))

    kt = jnp.concatenate([r[0].astype(BF16) for r in kv_refs], axis=1)
    vt = jnp.concatenate([r[1].astype(BF16) for r in kv_refs], axis=1)
    a, r_col = sweep(_dot(q4, kt), None, r_s[...])
    r_s[...] = r_col
    acc_s[...] = acc_s[...] + _dot_nt(a.astype(BF16), vt)

    @pl.when(step == pl.num_programs(1) - 1)
    def _():
        o_ref[0] = _unstack_heads(acc_s[...], H_STK, tqn)


def _stick_decode(q3, cache5, layer, page_table, knew, g_pages):
    db, n_pages = page_table.shape
    page = cache5.shape[4]
    tqn = q3.shape[1]
    ns = n_pages // g_pages
    rows = H_STK * tqn
    kv_specs = [pl.BlockSpec((None, None, 2, W_STK, page),
                             lambda b, s, pt, i=i: (layer, pt[b, n_pages - 1 - (s * g_pages + i)], 0, 0, 0))
                for i in range(g_pages)]
    grid_spec = pltpu.PrefetchScalarGridSpec(
        num_scalar_prefetch=1,
        grid=(db, ns),
        in_specs=kv_specs + [
            pl.BlockSpec((1, tqn, W_STK), lambda b, s, pt: (b, 0, 0)),
            pl.BlockSpec((1, 2, W_STK, page), lambda b, s, pt: (b, 0, 0, 0)),
        ],
        out_specs=pl.BlockSpec((1, tqn, W_STK), lambda b, s, pt: (b, 0, 0)),
        scratch_shapes=[pltpu.VMEM((rows, 1), F32), pltpu.VMEM((rows, W_STK), F32)],
    )
    return pl.pallas_call(
        functools.partial(_stick_decode_kernel, g_pages=g_pages),
        grid_spec=grid_spec,
        out_shape=jax.ShapeDtypeStruct((db, tqn, W_STK), F32),
        compiler_params=_cparams("arbitrary", "arbitrary"),
        name="stick_decode",
    )(page_table, *([cache5] * g_pages), q3, knew)


def _nsa_sample_small_kernel(q_ref, kc_ref, vc_ref, win_ref, oc_ref, ow_ref, selm_ref, *, nb, pos0, win_pos0):
    q = q_ref[0]
    tqn = q.shape[0]
    nbp = kc_ref.shape[1]
    qs = _stack_nsa(q)
    rows = qs.shape[0]
    tpos = pos0 + (_iota((rows, 1), 0) & (tqn - 1))
    kc = kc_ref[0].astype(BF16)
    vc = vc_ref[0].astype(BF16)
    n_idx = _iota((rows, nbp), 1)
    cmask = ((n_idx << CMP_SHIFT) + (CMP_BLOCK - 1) <= tpos) & (n_idx < nb)
    pc = _masked_softmax_rows(_dot_nt(qs, kc), cmask)
    oc_ref[0] = _unstack_nsa(_dot(pc.astype(BF16), vc), tqn)

    imp = []
    for g in range(G_NSA):
        acc = jnp.zeros((tqn, nbp), F32)
        for c in range(R_NSA):
            blk = 2 * c + g
            acc = acc + pc[blk * tqn:(blk + 1) * tqn]
        imp.append(acc)
    imp = jnp.concatenate(imp, axis=0)
    tp2 = pos0 + (_iota((G_NSA * tqn, 1), 0) & (tqn - 1))
    n2 = _iota(imp.shape, 1)
    visible = ((n2 << CMP_SHIFT) <= tp2) & (n2 < nb)
    cur = tp2 >> CMP_SHIFT
    forced = (n2 == 0) | (n2 == cur) | (n2 == cur - 1)
    score = jnp.where(visible, imp + FORCE_BONUS * jnp.where(forced, 1.0, 0.0), -jnp.inf)
    selected = _block_select(score, visible, n2, nb, min(TOP_N, nb), nbp)
    selm_ref[0] = jnp.where(selected, 1.0, 0.0)

    kw = win_ref[0, :, :W_GRP].astype(BF16)
    vw = win_ref[0, :, W_GRP:].astype(BF16)
    kpos = win_pos0 + _iota((1, kw.shape[0]), 1)
    pw = _masked_softmax_rows(_dot_nt(qs, kw), (kpos <= tpos) & (kpos > tpos - WINDOW) & (kpos >= win_pos0))
    ow_ref[0] = _unstack_nsa(_dot(pw.astype(BF16), vw), tqn)


def _nsa_sample_small(q3, kc, vc, win, nb, pos0, win_pos0):
    db, tqn, _ = q3.shape
    nbp = kc.shape[1]
    wk = win.shape[1]
    return pl.pallas_call(
        functools.partial(_nsa_sample_small_kernel, nb=nb, pos0=pos0, win_pos0=win_pos0),
        grid=(db,),
        in_specs=[
            pl.BlockSpec((1, tqn, W_NSA), lambda b: (b, 0, 0)),
            pl.BlockSpec((1, nbp, LANES), lambda b: (b, 0, 0)),
            pl.BlockSpec((1, nbp, LANES), lambda b: (b, 0, 0)),
            pl.BlockSpec((1, wk, 2 * W_GRP), lambda b: (b, 0, 0)),
        ],
        out_specs=(pl.BlockSpec((1, tqn, W_NSA), lambda b: (b, 0, 0)),
                   pl.BlockSpec((1, tqn, W_NSA), lambda b: (b, 0, 0)),
                   pl.BlockSpec((1, G_NSA * tqn, nbp), lambda b: (b, 0, 0))),
        out_shape=(jax.ShapeDtypeStruct((db, tqn, W_NSA), F32), jax.ShapeDtypeStruct((db, tqn, W_NSA), F32),
                   jax.ShapeDtypeStruct((db, G_NSA * tqn, nbp), F32)),
        compiler_params=_cparams("arbitrary"),
        name="nsa_sample_small",
    )(q3, kc, vc, win)


def _sel_decode_kernel(pt_ref, *refs, g_pages, nb_new):
    kv_refs = refs[:g_pages]
    (q_ref, knew_ref, msel_ref, mnew_ref, ee_ref, oc_ref, ow_ref, misc_ref, eg_ref,
     o_ref, m_s, l_s, acc_s) = refs[g_pages:]
    step = pl.program_id(1)
    tqn = q_ref.shape[1]
    qs = _stack_nsa(q_ref[0])
    rows = qs.shape[0]
    tqv = _iota((rows, 1), 0) & (tqn - 1)

    @pl.when(step == 0)
    def _():
        m_s[...] = jnp.full(m_s.shape, -jnp.inf, F32)
        l_s[...] = jnp.zeros(l_s.shape, F32)
        acc_s[...] = jnp.zeros(acc_s.shape, F32)
        s = _dot(qs, knew_ref[0, 0].astype(BF16))
        lane = _iota(s.shape, 1)
        mn = mnew_ref[0][:, nb_new:nb_new + 1] > 0.5
        mn = jnp.concatenate([mn] * R_NSA, axis=0)
        _softmax_update(s, mn & (lane <= tqv), m_s, l_s, acc_s, knew_ref[0, 1].astype(BF16))

    kt = jnp.concatenate([r[0].astype(BF16) for r in kv_refs], axis=1)
    vt = jnp.concatenate([r[1].astype(BF16) for r in kv_refs], axis=1)
    s = _dot(qs, kt)
    me = _dot(msel_ref[0, 0].astype(BF16), ee_ref[...]) > 0.5
    _softmax_update(s, jnp.concatenate([me] * R_NSA, axis=0), m_s, l_s, acc_s, vt)

    @pl.when(step == pl.num_programs(1) - 1)
    def _():
        o_s = _unstack_nsa(acc_s[...] / jnp.maximum(l_s[...], 1e-30), tqn)
        g_c, g_s, g_w = _gate_expand(misc_ref[0], eg_ref)
        o_ref[0] = g_c * oc_ref[0] + g_s * o_s + g_w * ow_ref[0]


def _sel_decode(q3, cache5, layer, page_table, knew, selm, o_c, o_w, misc3, consts, g_pages):
    db, n_pages = page_table.shape
    page = cache5.shape[4]
    tqn = q3.shape[1]
    ns = n_pages // g_pages
    rows = H_NSA * tqn
    nbp = selm.shape[2]
    bps = g_pages * page // CMP_BLOCK
    nb_past = n_pages * page // CMP_BLOCK
    msel = selm[:, :, :nb_past].reshape(db, G_NSA * tqn, ns, bps).transpose(0, 2, 1, 3)
    kv_specs = [pl.BlockSpec((None, None, 2, W_GRP, page),
                             lambda b, s, pt, i=i: (layer, pt[b, s * g_pages + i], 0, 0, 0))
                for i in range(g_pages)]
    per_b = lambda b, s, pt: (b, 0, 0)
    grid_spec = pltpu.PrefetchScalarGridSpec(
        num_scalar_prefetch=1,
        grid=(db, ns),
        in_specs=kv_specs + [
            pl.BlockSpec((1, tqn, W_NSA), per_b),
            pl.BlockSpec((1, 2, W_GRP, page), lambda b, s, pt: (b, 0, 0, 0)),
            pl.BlockSpec((1, 1, G_NSA * tqn, bps), lambda b, s, pt: (b, s, 0, 0)),
            pl.BlockSpec((1, G_NSA * tqn, nbp), per_b),
            pl.BlockSpec((bps, g_pages * page), lambda b, s, pt: (0, 0)),
            pl.BlockSpec((1, tqn, W_NSA), per_b),
            pl.BlockSpec((1, tqn, W_NSA), per_b),
            pl.BlockSpec((1, tqn, LANES), per_b),
            pl.BlockSpec((3, LANES, W_NSA), lambda b, s, pt: (0, 0, 0)),
        ],
        out_specs=pl.BlockSpec((1, tqn, W_NSA), per_b),
        scratch_shapes=[pltpu.VMEM((rows, 1), F32), pltpu.VMEM((rows, 1), F32), pltpu.VMEM((rows, LANES), F32)],
    )
    return pl.pallas_call(
        functools.partial(_sel_decode_kernel, g_pages=g_pages, nb_new=nb_past),
        grid_spec=grid_spec,
        out_shape=jax.ShapeDtypeStruct((db, tqn, W_NSA), F32),
        compiler_params=_cparams("arbitrary", "arbitrary"),
        name="nsa_sel_decode",
    )(page_table, *([cache5] * g_pages), q3, knew, msel, selm, consts["ee_decode"], o_c, o_w, misc3,
      consts["egate"])


def _mix_ffn_kernel(x_ref, of_ref, on_ref, os_ref, gmix_ref, wo_ref, gffn_ref, wg_ref, wu_ref, wd_ref, o_ref,
                    x1_s, xn_s, acc_s):
    j = pl.program_id(1)

    def rms(y, g):
        return y * lax.rsqrt(jnp.mean(y * y, axis=-1, keepdims=True) + EPS) * g

    @pl.when(j == 0)
    def _():
        o = jnp.concatenate([
            rms(of_ref[...], gmix_ref[:, :W_FOX]),
            rms(on_ref[...], gmix_ref[:, W_FOX:W_FOX + W_NSA]),
            rms(os_ref[...], gmix_ref[:, W_FOX + W_NSA:]),
        ], axis=1).astype(BF16)
        x1 = x_ref[...] + _dot(o, wo_ref[...])
        x1_s[...] = x1
        xn_s[...] = rms(x1, gffn_ref[...]).astype(BF16)
        acc_s[...] = jnp.zeros(acc_s.shape, F32)

    xn = xn_s[...]
    hg = _dot(xn, wg_ref[...])
    hu = _dot(xn, wu_ref[...])
    h = (hg * (1.0 / (1.0 + jnp.exp(-hg))) * hu).astype(BF16)
    acc_s[...] = acc_s[...] + _dot(h, wd_ref[...])

    @pl.when(j == pl.num_programs(1) - 1)
    def _():
        o_ref[...] = x1_s[...] + acc_s[...]


def _mix_ffn(x2d, o_fox, o_nsa, o_stk, lw, tn, tf):
    n, d = x2d.shape
    dff = lw["w_gate"].shape[1]
    row = lambda i, j: (i, 0)
    const = lambda i, j: (0, 0)
    return pl.pallas_call(
        _mix_ffn_kernel,
        grid=(n // tn, dff // tf),
        in_specs=[
            pl.BlockSpec((tn, d), row),
            pl.BlockSpec((tn, W_FOX), row),
            pl.BlockSpec((tn, W_NSA), row),
            pl.BlockSpec((tn, W_STK), row),
            pl.BlockSpec((1, d), const),
            pl.BlockSpec((d, d), const),
            pl.BlockSpec((1, d), const),
            pl.BlockSpec((d, tf), lambda i, j: (0, j)),
            pl.BlockSpec((d, tf), lambda i, j: (0, j)),
            pl.BlockSpec((tf, d), lambda i, j: (j, 0)),
        ],
        out_specs=pl.BlockSpec((tn, d), row),
        out_shape=jax.ShapeDtypeStruct((n, d), F32),
        scratch_shapes=[pltpu.VMEM((tn, d), F32), pltpu.VMEM((tn, d), BF16), pltpu.VMEM((tn, d), F32)],
        compiler_params=_cparams("arbitrary", "arbitrary"),
        name="mix_ffn",
    )(x2d, o_fox, o_nsa, o_stk, lw["g_mix"], lw["w_o"], lw["g_ffn"], lw["w_gate"], lw["w_up"], lw["w_down"])


def _nsa_lane_perm():
    perm = []
    for h in NSA_HEAD_ORDER:
        perm.extend(range(h * HEAD_DIM, (h + 1) * HEAD_DIM))
    return np.asarray(perm, np.int32)


def _layer_weights(l, w):
    d = w["w_in"].shape[1]
    perm = _nsa_lane_perm()
    wt = jnp.swapaxes(w["w_in"], 1, 2)[l]
    o_fq, o_fk, o_ff, o_nq = 0, W_FOX, 3 * W_FOX, 3 * W_FOX + H_FOX
    o_kc = o_nq + W_NSA
    o_ng = o_kc + 6 * W_GRP
    o_sq = o_ng + 3 * H_NSA
    w_perm = jnp.concatenate(
        [wt[o_fq:o_fq + W_FOX], wt[o_fk:o_fk + 2 * W_FOX]]
        + [wt[o_nq + h * HEAD_DIM:o_nq + (h + 1) * HEAD_DIM] for h in NSA_HEAD_ORDER]
        + [wt[o_kc:o_kc + 6 * W_GRP], wt[o_sq:o_sq + W_STK], wt[o_sq + W_STK:o_sq + 3 * W_STK],
           wt[o_ff:o_ff + H_FOX], wt[o_ng:o_ng + 3 * H_NSA], jnp.zeros((LANES - N_MISC, d), F32)],
        axis=0).astype(BF16)
    bias = jnp.concatenate([w["b_fox_f"][l], w["b_nsa_gate"][l], jnp.zeros((LANES - N_MISC,), F32)])[None, :]

    def tiled(g, width):
        return jnp.pad(jnp.tile(g, width // HEAD_DIM), (0, W_NSA - width))

    gains = jnp.stack([tiled(w["fox_gq"][l], W_FOX), tiled(w["fox_gk"][l], W_FOX), tiled(w["nsa_gq"][l], W_NSA),
                       tiled(w["nsa_gks"][l], W_GRP), tiled(w["nsa_gkw"][l], W_GRP)]
                      + [jnp.zeros((W_NSA,), F32)] * 3)
    s64 = jnp.asarray(np.kron(np.eye(4, dtype=np.float32), np.full((64, 64), 1.0 / 64, np.float32)), BF16)

    def w2pad(w2):
        z = jnp.zeros_like(w2)
        return jnp.stack([jnp.concatenate([w2, z], axis=1), jnp.concatenate([z, w2], axis=1)]).astype(BF16)

    def by_feature(w1):
        return w1.reshape(CMP_BLOCK, HEAD_DIM, CMP_HIDDEN).transpose(1, 0, 2).reshape(CMP_BLOCK * HEAD_DIM, CMP_HIDDEN).astype(BF16)

    g_mix = w["g_mix"][l]
    g_mix = jnp.concatenate([g_mix[:W_FOX], g_mix[W_FOX:W_FOX + W_NSA][perm], g_mix[W_FOX + W_NSA:]])[None, :]
    w_o = w["w_o"][l]
    w_o = jnp.concatenate([w_o[:W_FOX], w_o[W_FOX:W_FOX + W_NSA][perm], w_o[W_FOX + W_NSA:]], axis=0).astype(BF16)
    return {
        "g_attn": w["g_attn"][l][None, :], "w_in": w_perm, "bias_misc": bias, "gains": gains, "s64": s64,
        "pe4": jnp.stack([w["nsa_pe_k"][l].reshape(CMP_BLOCK // 4, 256), w["nsa_pe_v"][l].reshape(CMP_BLOCK // 4, 256)]),
        "w1k": w["nsa_w1k"][l].astype(BF16), "w1v": w["nsa_w1v"][l].astype(BF16),
        "pe4t": jnp.stack([w["nsa_pe_k"][l].T.reshape(HEAD_DIM // 4, 256), w["nsa_pe_v"][l].T.reshape(HEAD_DIM // 4, 256)]),
        "w1kt": by_feature(w["nsa_w1k"][l]), "w1vt": by_feature(w["nsa_w1v"][l]),
        "w2k": w2pad(w["nsa_w2k"][l]), "w2v": w2pad(w["nsa_w2v"][l]),
        "gkc": jnp.tile(w["nsa_gkc"][l], 2)[None, :],
        "g_mix": g_mix, "w_o": w_o, "g_ffn": w["g_ffn"][l][None, :],
        "w_gate": w["w_gate"][l].astype(BF16), "w_up": w["w_up"][l].astype(BF16), "w_down": w["w_down"][l].astype(BF16),
    }


def _constants(t, tk, g_sel, page):
    perm = _nsa_lane_perm()
    egate = np.zeros((3, LANES, W_NSA), np.float32)
    for lane in range(W_NSA):
        h = int(perm[lane]) // HEAD_DIM
        for j in range(3):
            egate[j, H_FOX + 3 * h + j, lane] = 1.0
    nb = t // CMP_BLOCK
    eb = np.zeros((G_NSA, LANES, t), np.float32)
    for g in range(G_NSA):
        for n in range(nb):
            eb[g, g * HEAD_DIM + n, n * CMP_BLOCK:(n + 1) * CMP_BLOCK] = 1.0
    bps = g_sel * page // CMP_BLOCK
    ee = np.zeros((bps, g_sel * page), np.float32)
    for n in range(bps):
        ee[n, n * CMP_BLOCK:(n + 1) * CMP_BLOCK] = 1.0
    eb = eb.reshape(G_NSA, LANES, t // tk, tk).transpose(0, 2, 1, 3)
    return {"egate": jnp.asarray(egate, BF16), "eb_prompt": jnp.asarray(eb, BF16), "ee_decode": jnp.asarray(ee, BF16)}


def _nsa_tk(t):
    return 2 * LANES if t % (2 * LANES) == 0 else LANES


def _pick(n, prefs):
    for p in prefs:
        if n % p == 0:
            return p
    return n


def _prompt_layer(x3, lw, tabs, consts):
    b, t, d = x3.shape
    n = b * t
    tn = _pick(t, (256, 128))
    (qf, fkv, qn, cmpr, sel, win, qs, skv, misc) = _proj(x3.reshape(n, d), lw, tabs["cos_p"], tabs["sin_p"], t // tn, tn)
    tq = _pick(t, (256, 128))
    tk = tq
    misc3 = misc.reshape(b, t, LANES)
    ccol, crow = _cumsum(misc3, tk)
    o_fox = _fox_prompt(qf.reshape(b, t, W_FOX), fkv.reshape(b, t, 2 * W_FOX), ccol, crow, tq, tk)
    o_stk = _stick_prompt(qs.reshape(b, t, W_STK), skv.reshape(b, t, 2 * W_STK), tq, tk)
    kc, vc = _compress_contig(cmpr.reshape(b, t, 2 * W_GRP), lw, tabs["cos_bp"], tabs["sin_bp"])
    o_nsa = _nsa_prompt(qn.reshape(b, t, W_NSA), kc, vc, sel.reshape(b, t, 2 * W_GRP), win.reshape(b, t, 2 * W_GRP),
                        misc3, consts, LANES, _nsa_tk(t))
    tnf = _pick(n, (512, 256, 128))
    x_out = _mix_ffn(x3.reshape(n, d), o_fox.reshape(n, W_FOX), o_nsa.reshape(n, W_NSA), o_stk.reshape(n, W_STK),
                     lw, tnf, _pick(lw["w_gate"].shape[1], (1408, 1024, 512, 256, 128)))
    n_win = min(WINDOW, t)
    rows = (fkv.reshape(b, t, 2, H_FOX, HEAD_DIM), misc3[:, :, :H_FOX], cmpr.reshape(b, t, 2, G_NSA, HEAD_DIM),
            sel.reshape(b, t, 2, G_NSA, HEAD_DIM), win.reshape(b, t, 2, G_NSA, HEAD_DIM)[:, t - n_win:],
            skv.reshape(b, t, 2, H_STK, HEAD_DIM))
    return x_out.reshape(b, t, d), rows


def _sample_layer(x3, l, lw, tabs, consts, caches, page_table, g_pages):
    db, tqn, d = x3.shape
    n = db * tqn
    n_pages = page_table.shape[1]
    page = caches["fox"].shape[4]
    past = n_pages * page
    (qf, fkv, qn, cmpr, sel, win, qs, skv, misc) = _proj(x3.reshape(n, d), lw, tabs["cos_s"], tabs["sin_s"], 1, n)
    misc3 = misc.reshape(db, tqn, LANES)

    def new_page(rows3):
        w_ = rows3.shape[2] // 2
        r = rows3.reshape(db, tqn, 2, w_).transpose(0, 2, 3, 1)
        return jnp.pad(r, ((0, 0), (0, 0), (0, 0), (0, page - tqn)))

    fkv3 = fkv.reshape(db, tqn, 2 * W_FOX)
    lfn = jnp.pad(jnp.swapaxes(misc3[:, :, :H_FOX], 1, 2), ((0, 0), (0, 8 - H_FOX), (0, LANES - tqn)))
    o_fox = _fox_decode(qf.reshape(db, tqn, W_FOX), caches["fox"], caches["logf"], l, page_table,
                        new_page(fkv3), lfn, g_pages["fox"])
    skv3 = skv.reshape(db, tqn, 2 * W_STK)
    o_stk = _stick_decode(qs.reshape(db, tqn, W_STK), caches["stk"], l, page_table, new_page(skv3), g_pages["stk"])

    cmp3 = cmpr.reshape(db, tqn, 2 * W_GRP)
    kc_p, vc_p = _compress_paged(caches["cmp"], l, page_table, lw, tabs["cos_bs"], tabs["sin_bs"], g_pages["cmp"])
    new_blk = jnp.pad(cmp3, ((0, 0), (0, CMP_BLOCK - tqn), (0, 0))).reshape(1, db * CMP_BLOCK, 2 * W_GRP)
    kc_n, vc_n = _compress_contig(new_blk, lw, tabs["cos_bn"], tabs["sin_bn"])
    nb = past // CMP_BLOCK + 1
    nbp = -(-nb // LANES) * LANES
    cat = lambda a, c: jnp.pad(jnp.concatenate([a, c.reshape(db, 1, LANES)], axis=1), ((0, 0), (0, nbp - nb), (0, 0)))
    kc = cat(kc_p, kc_n)
    vc = cat(vc_p, vc_n)
    win3 = win.reshape(db, tqn, 2 * W_GRP)
    win_all = jnp.concatenate([caches["win"][l], win3], axis=1)
    wk = win_all.shape[1]
    wkp = -(-wk // LANES) * LANES
    q3 = qn.reshape(db, tqn, W_NSA)
    o_c, o_w, selm = _nsa_sample_small(q3, kc, vc, jnp.pad(win_all, ((0, 0), (0, wkp - wk), (0, 0))),
                                       nb, past, past + tqn - wk)
    sel3 = sel.reshape(db, tqn, 2 * W_GRP)
    o_nsa = _sel_decode(q3, caches["sel"], l, page_table, new_page(sel3), selm, o_c, o_w, misc3, consts, g_pages["sel"])

    x_out = _mix_ffn(x3.reshape(n, d), o_fox.reshape(n, W_FOX), o_nsa.reshape(n, W_NSA), o_stk.reshape(n, W_STK),
                     lw, n, _pick(lw["w_gate"].shape[1], (1408, 1024, 512, 256, 128)))
    n_win = caches["win"].shape[2]
    rows = (fkv.reshape(db, tqn, 2, H_FOX, HEAD_DIM), misc3[:, :, :H_FOX], cmpr.reshape(db, tqn, 2, G_NSA, HEAD_DIM),
            sel.reshape(db, tqn, 2, G_NSA, HEAD_DIM), win_all[:, wk - n_win:].reshape(db, n_win, 2, G_NSA, HEAD_DIM),
            skv.reshape(db, tqn, 2, H_STK, HEAD_DIM))
    return x_out.reshape(db, tqn, d), rows


def kernel(x_prompt, x_sample, cache_fox_kv, cache_fox_logf, cache_nsa_cmp_kv, cache_nsa_sel_kv, state_nsa_win_kv, cache_stk_kv, page_table, g_attn, w_in, b_fox_f, b_nsa_gate, fox_gq, fox_gk, nsa_gq, nsa_gkc, nsa_gks, nsa_gkw, nsa_pe_k, nsa_w1k, nsa_w2k, nsa_pe_v, nsa_w1v, nsa_w2v, g_mix, w_o, g_ffn, w_gate, w_up, w_down):
    weights = dict(g_attn=g_attn, w_in=w_in, b_fox_f=b_fox_f, b_nsa_gate=b_nsa_gate, fox_gq=fox_gq, fox_gk=fox_gk,
                   nsa_gq=nsa_gq, nsa_gkc=nsa_gkc, nsa_gks=nsa_gks, nsa_gkw=nsa_gkw, nsa_pe_k=nsa_pe_k,
                   nsa_w1k=nsa_w1k, nsa_w2k=nsa_w2k, nsa_pe_v=nsa_pe_v, nsa_w1v=nsa_w1v, nsa_w2v=nsa_w2v,
                   g_mix=g_mix, w_o=w_o, g_ffn=g_ffn, w_gate=w_gate, w_up=w_up, w_down=w_down)
    depth = w_in.shape[0]
    b, t, d = x_prompt.shape
    db, tqn, _ = x_sample.shape
    n_pool, page = cache_fox_kv.shape[1], cache_fox_kv.shape[2]
    n_pages = page_table.shape[1]
    past = n_pages * page
    assert t % LANES == 0 and t // CMP_BLOCK <= HEAD_DIM and tqn == 8 and page == LANES
    n_win_s = state_nsa_win_kv.shape[2]

    def paged(cache, w_):
        return jnp.transpose(cache, (0, 1, 3, 4, 5, 2)).reshape(depth, n_pool, 2, w_, page)

    caches = {
        "fox": paged(cache_fox_kv, W_FOX),
        "logf": jnp.swapaxes(cache_fox_logf, 2, 3),
        "cmp": paged(cache_nsa_cmp_kv, W_GRP),
        "sel": paged(cache_nsa_sel_kv, W_GRP),
        "stk": paged(cache_stk_kv, W_STK),
        "win": state_nsa_win_kv.reshape(depth, db, n_win_s, 2 * W_GRP),
    }
    g_pages = {k: _pick(n_pages, (32, 16, 8, 4, 2, 1)) for k in ("fox", "stk", "sel")}
    g_pages["cmp"] = _pick(n_pages, (32, 16, 8, 4, 2, 1))
    consts = _constants(t, _nsa_tk(t), g_pages["sel"], page)

    nbp_blocks = t // CMP_BLOCK
    nbs_blocks = past // CMP_BLOCK
    bend = lambda n0, cnt: (n0 + jnp.arange(cnt, dtype=jnp.int32)) * CMP_BLOCK + (CMP_BLOCK - 1)
    pos_all = jnp.concatenate([
        jnp.arange(t, dtype=jnp.int32),
        jnp.tile(past + jnp.arange(tqn, dtype=jnp.int32), db),
        bend(0, nbp_blocks), bend(0, nbs_blocks).reshape(-1, g_pages["cmp"], 2).transpose(0, 2, 1).reshape(-1),
        jnp.tile(bend(nbs_blocks, 1), db),
    ])
    pad = (-pos_all.shape[0]) % 8
    cos_all, sin_all = _rope_tables(jnp.pad(pos_all, (0, pad)))
    offs = np.cumsum([0, t, db * tqn, nbp_blocks, nbs_blocks, db])
    names = ("p", "s", "bp", "bs", "bn")
    tabs = {}
    for i, nm in enumerate(names):
        tabs["cos_" + nm] = cos_all[offs[i]:offs[i + 1]]
        tabs["sin_" + nm] = sin_all[offs[i]:offs[i + 1]]

    xp, xs = x_prompt, x_sample
    rows_p, rows_s = [], []
    for l in range(depth):
        lw = _layer_weights(l, weights)
        xp, rp = _prompt_layer(xp, lw, tabs, consts)
        xs, rs = _sample_layer(xs, l, lw, tabs, consts, caches, page_table, g_pages)
        rows_p.append(rp)
        rows_s.append(rs)
    outs = [xp, xs]
    for i in range(6):
        outs.append(jnp.stack([r[i] for r in rows_p], axis=0))
        outs.append(jnp.stack([r[i] for r in rows_s], axis=0))
    return tuple(outs)
```
